```python
import jax
import jax.numpy as jnp
from jax import lax
import numpy as np

D_MODEL = 4096
BATCH = 2
SEQ = 8192
DEPTH = 4

N_EVEN = (DEPTH + 1) // 2
N_ODD = DEPTH // 2
ATTN_WIDTH = D_MODEL // 2
HEAD_DIM = 64
N_Q_HEADS = ATTN_WIDTH // HEAD_DIM
N_KV_HEADS = N_Q_HEADS // 8
KV_WIDTH = N_KV_HEADS * HEAD_DIM
WINDOW = 128
ROT_DIM = HEAD_DIM // 4
ROPE_THETA = 500000.0
MASK_VALUE = -1e9
HGRN_WIDTH = D_MODEL - ATTN_WIDTH
HGRN_DIM = 128
N_HGRN_HEADS = HGRN_WIDTH // HGRN_DIM
HGRN_CHUNK = 64
EVEN_IN_COLS = ATTN_WIDTH + 2 * KV_WIDTH + 4 * HGRN_WIDTH
LRU_WIDTH = D_MODEL
N_LRU_HEADS = 16
LRU_HEAD_DIM = LRU_WIDTH // N_LRU_HEADS
CONV_WIDTH = 4
RG_C = 8.0
N_GROUPS = 4
EXPERTS_PER_GROUP = 8
N_EXPERTS = N_GROUPS * EXPERTS_PER_GROUP
TOP_K = 2
EXPERT_FF = 3 * D_MODEL // 32
DISPATCH_BLOCK = 128
ALPHA = (2.0 * DEPTH) ** 0.25
BETA = (8.0 * DEPTH) ** -0.25
LN_EPS = 1e-5
RMS_EPS = 1e-6

kernel_name = 'hybrid_swa_hgrn2_rglru_hmoe_deepnorm'


def _split_cols(t, widths):
    out, off = [], 0
    for w in widths:
        out.append(t[..., off:off + w])
        off += w
    return out


def layer_norm(x, g, b):
    xf = x.astype(jnp.float32)
    mu = jnp.mean(xf, axis=-1, keepdims=True)
    var = jnp.mean(jnp.square(xf - mu), axis=-1, keepdims=True)
    return ((xf - mu) * lax.rsqrt(var + LN_EPS) * g + b).astype(x.dtype)


def partial_rotary(t, cos, sin):
    half = ROT_DIM // 2
    t1, t2, rest = t[..., :half], t[..., half:ROT_DIM], t[..., ROT_DIM:]
    return jnp.concatenate([t1 * cos - t2 * sin, t2 * cos + t1 * sin, rest], axis=-1)


def swa_with_sinks(q, k, v, sinks):
    B, S = q.shape[0], q.shape[1]
    nb = S // WINDOW
    G = N_Q_HEADS // N_KV_HEADS
    qb = q.reshape(B, nb, WINDOW, N_KV_HEADS, G, HEAD_DIM)

    def band(t):
        tb = t.reshape(B, nb, WINDOW, N_KV_HEADS, HEAD_DIM)
        prev = jnp.pad(tb, ((0, 0), (1, 0), (0, 0), (0, 0), (0, 0)))[:, :-1]
        return jnp.concatenate([prev, tb], axis=2)

    kw, vw = band(k), band(v)
    s = jnp.einsum('bnqhgd,bnkhd->bnhgqk', qb, kw) * (HEAD_DIM ** -0.5)
    qpos = jnp.arange(WINDOW)[:, None] + WINDOW
    kpos = jnp.arange(2 * WINDOW)[None, :]
    rel = qpos - kpos
    in_band = (rel >= 0) & (rel < WINDOW)
    has_prev = (jnp.arange(nb) > 0)[:, None, None] | (kpos >= WINDOW)[None]
    mask = (in_band[None] & has_prev)[None, :, None, None]
    s = jnp.where(mask, s, MASK_VALUE)
    sink = sinks.astype(jnp.float32).reshape(1, 1, N_KV_HEADS, G, 1, 1)
    m = jnp.maximum(jnp.max(s, axis=-1, keepdims=True), sink)
    p = jnp.where(mask, jnp.exp(s - m), 0.0)
    denom = jnp.sum(p, axis=-1, keepdims=True) + jnp.exp(sink - m)
    o = jnp.einsum('bnhgqk,bnkhd->bnqhgd', p / denom, vw)
    return o.reshape(B, S, N_Q_HEADS, HEAD_DIM)


def hgrn2_chunked(q, k, v, log_f):
    B, S, H, DK = q.shape
    DV = v.shape[-1]
    nc = S // HGRN_CHUNK

    def to_chunks(t):
        return t.reshape(B, nc, HGRN_CHUNK, H, t.shape[-1]).transpose(1, 0, 3, 2, 4)

    causal = jnp.tril(jnp.ones((HGRN_CHUNK, HGRN_CHUNK), dtype=bool))[:, :, None]

    def step(state, inp):
        qc, kc, vc, lf = inp
        b = jnp.cumsum(lf, axis=2)
        diff = b[:, :, :, None, :] - b[:, :, None, :, :]
        decay = jnp.where(causal, jnp.exp(jnp.where(causal, diff, 0.0)), 0.0)
        scores = jnp.einsum('bhtk,bhsk,bhtsk->bhts', qc, kc, decay)
        o = jnp.einsum('bhts,bhsv->bhtv', scores, vc) + jnp.einsum('bhtk,bhkv->bhtv', qc * jnp.exp(b), state)
        b_last = b[:, :, -1:, :]
        new_state = jnp.exp(b_last[:, :, 0, :])[..., None] * state + jnp.einsum('bhsk,bhsv->bhkv', kc * jnp.exp(b_last - b), vc)
        return new_state, o

    s0 = jnp.zeros((B, H, DK, DV), jnp.float32)
    _, o = lax.scan(step, s0, (to_chunks(q), to_chunks(k), to_chunks(v), to_chunks(log_f)))
    return o.transpose(1, 0, 3, 2, 4).reshape(B, S, H, DV)


def even_mixer(x, cos, sin, w_in, w_out, sinks, lb, gnorm_w):
    B, S, _ = x.shape
    proj = jnp.einsum('bsd,de->bse', x, w_in).astype(jnp.float32)
    q_a, k_a, v_a, q_b, f_b, i_b, g_b = _split_cols(
        proj, (ATTN_WIDTH, KV_WIDTH, KV_WIDTH, HGRN_WIDTH, HGRN_WIDTH, HGRN_WIDTH, HGRN_WIDTH))
    q_a = partial_rotary(q_a.reshape(B, S, N_Q_HEADS, HEAD_DIM), cos, sin)
    k_a = partial_rotary(k_a.reshape(B, S, N_KV_HEADS, HEAD_DIM), cos, sin)
    o_a = swa_with_sinks(q_a, k_a, v_a.reshape(B, S, N_KV_HEADS, HEAD_DIM), sinks)
    def hd(t):
        return t.reshape(B, S, N_HGRN_HEADS, HGRN_DIM)
    lbf = lb.astype(jnp.float32).reshape(N_HGRN_HEADS, HGRN_DIM)
    z = hd(f_b)
    f = lbf + (1.0 - lbf) * jax.nn.sigmoid(z)
    log_f = jnp.log(f)
    k_b = (1.0 - lbf) * jax.nn.sigmoid(-z)
    o_b = hgrn2_chunked(jax.nn.silu(hd(q_b)), k_b, hd(i_b), log_f)
    o_b = o_b * lax.rsqrt(jnp.mean(jnp.square(o_b), axis=-1, keepdims=True) + RMS_EPS) * gnorm_w.astype(jnp.float32) * jax.nn.silu(hd(g_b))
    mixed = jnp.concatenate([o_a.reshape(B, S, ATTN_WIDTH), o_b.reshape(B, S, HGRN_WIDTH)], axis=-1)
    return jnp.einsum('bse,ed->bsd', mixed.astype(x.dtype), w_out)


def rg_lru(u, wa, ba, wx, bx, lam):
    B, S, _ = u.shape
    uh = u.reshape(B, S, N_LRU_HEADS, LRU_HEAD_DIM)
    r = jax.nn.sigmoid(jnp.einsum('bshi,hij->bshj', uh, wa).reshape(B, S, LRU_WIDTH) + ba)
    ig = jax.nn.sigmoid(jnp.einsum('bshi,hij->bshj', uh, wx).reshape(B, S, LRU_WIDTH) + bx)
    log_a = -RG_C * r * jax.nn.softplus(-lam.astype(jnp.float32))
    a = jnp.exp(log_a)
    b = jnp.sqrt(jnp.maximum(-jnp.expm1(2.0 * log_a), 0.0)) * (ig * u)

    def combine(left, right):
        a_l, b_l = left
        a_r, b_r = right
        return a_l * a_r, a_r * b_l + b_r

    _, h = lax.associative_scan(combine, (a, b), axis=1)
    return h


def recurrent_mixer(x, w_in, conv_w, conv_b, wa, ba, wx, bx, lam, w_out):
    B, S, _ = x.shape
    proj = jnp.einsum('bsd,de->bse', x, w_in).astype(jnp.float32)
    gate_br, rnn_br = proj[..., :LRU_WIDTH], proj[..., LRU_WIDTH:]
    gate_br = jax.nn.gelu(gate_br)
    xpad = jnp.pad(rnn_br, ((0, 0), (CONV_WIDTH - 1, 0), (0, 0)))
    conv = conv_b.astype(jnp.float32) + xpad[:, 0:S] * conv_w[0]
    for j in range(1, CONV_WIDTH):
        conv = conv + xpad[:, j:j + S] * conv_w[j]
    h = rg_lru(conv, wa, ba, wx, bx, lam)
    return jnp.einsum('bse,ed->bsd', (h * gate_br).astype(x.dtype), w_out)


def hierarchical_moe(x, wg, bg, we, be, w_gate, w_up, w_down):
    B, S, D = x.shape
    xf = x.reshape(-1, D)
    N = xf.shape[0]
    xr = xf.astype(jnp.float32)
    p_group = jax.nn.softmax(xr @ wg.astype(jnp.float32) + bg, axis=-1)
    p_top, g_sel = lax.top_k(p_group, 1)
    e_logits = (xr @ we.astype(jnp.float32) + be).reshape(N, N_GROUPS, EXPERTS_PER_GROUP)
    e_logits = e_logits[jnp.arange(N), g_sel[:, 0]]
    w_top, e_local = lax.top_k(jax.nn.softmax(e_logits, axis=-1), TOP_K)
    w_top = w_top / jnp.sum(w_top, axis=-1, keepdims=True) * p_top
    e_idx = g_sel * EXPERTS_PER_GROUP + e_local
    A = N * TOP_K
    n_blocks = (A + N_EXPERTS * (DISPATCH_BLOCK - 1) + DISPATCH_BLOCK - 1) // DISPATCH_BLOCK
    n_slots = n_blocks * DISPATCH_BLOCK
    flat_e = e_idx.reshape(-1)
    onehot = jax.nn.one_hot(flat_e, N_EXPERTS, dtype=jnp.int32)
    counts = jnp.sum(onehot, axis=0)
    rank = jnp.sum((jnp.cumsum(onehot, axis=0) - 1) * onehot, axis=1)
    padded = (counts + DISPATCH_BLOCK - 1) // DISPATCH_BLOCK * DISPATCH_BLOCK
    pad_end = jnp.cumsum(padded)
    dest = pad_end[flat_e] - padded[flat_e] + rank
    slot_tok = jnp.full((n_slots,), N, jnp.int32).at[dest].set(jnp.arange(A, dtype=jnp.int32) // TOP_K)
    slot_w = jnp.zeros((n_slots,), jnp.float32).at[dest].set(w_top.reshape(-1))
    block_e = jnp.minimum(jnp.searchsorted(pad_end, jnp.arange(n_blocks) * DISPATCH_BLOCK, side='right'), N_EXPERTS - 1)
    x_pad = jnp.concatenate([xf, jnp.zeros((1, D), xf.dtype)], axis=0)

    def run_block(args):
        toks, e = args
        xb = x_pad[toks]
        hid = jax.nn.silu(xb @ w_gate[e]) * (xb @ w_up[e])
        return hid @ w_down[e]

    y = lax.map(run_block, (slot_tok.reshape(n_blocks, DISPATCH_BLOCK), block_e))
    y = y.reshape(n_slots, D).astype(jnp.float32) * slot_w[:, None]
    out = jax.ops.segment_sum(y, slot_tok, num_segments=N + 1)[:N]
    return out.reshape(B, S, D).astype(x.dtype)


def setup_inputs(seed: int = 0) -> dict:
    key = jax.random.key(seed)
    ks = jax.random.split(key, 32)
    f32 = jnp.float32

    def nrm(k, shape, scale):
        return jax.random.normal(k, shape, f32) * scale

    u = jax.random.uniform(ks[15], (N_ODD, LRU_WIDTH), f32, 0.9, 0.999)
    s_lam = u ** (1.0 / RG_C)
    return {
        'x': nrm(ks[0], (BATCH, SEQ, D_MODEL), 1.0),
        'positions': jnp.broadcast_to(jnp.arange(SEQ, dtype=jnp.int32), (BATCH, SEQ)),
        'even_w_in': nrm(ks[1], (N_EVEN, D_MODEL, EVEN_IN_COLS), D_MODEL ** -0.5),
        'even_w_out': nrm(ks[2], (N_EVEN, ATTN_WIDTH + HGRN_WIDTH, D_MODEL), (ATTN_WIDTH + HGRN_WIDTH) ** -0.5 * BETA),
        'attn_sinks': nrm(ks[3], (N_EVEN, N_Q_HEADS), 1.0),
        'hgrn_lb_logits': nrm(ks[4], (N_EVEN, HGRN_WIDTH), 0.1),
        'hgrn_gnorm_w': 1.0 + nrm(ks[5], (N_EVEN, HGRN_DIM), 0.01),
        'rec_w_in': nrm(ks[6], (N_ODD, D_MODEL, 2 * LRU_WIDTH), D_MODEL ** -0.5),
        'rec_conv_w': nrm(ks[7], (N_ODD, CONV_WIDTH, LRU_WIDTH), CONV_WIDTH ** -0.5),
        'rec_conv_b': nrm(ks[8], (N_ODD, LRU_WIDTH), 0.01),
        'rec_gate_a_w': nrm(ks[9], (N_ODD, N_LRU_HEADS, LRU_HEAD_DIM, LRU_HEAD_DIM), LRU_HEAD_DIM ** -0.5),
        'rec_gate_a_b': nrm(ks[10], (N_ODD, LRU_WIDTH), 0.01),
        'rec_gate_x_w': nrm(ks[11], (N_ODD, N_LRU_HEADS, LRU_HEAD_DIM, LRU_HEAD_DIM), LRU_HEAD_DIM ** -0.5),
        'rec_gate_x_b': nrm(ks[12], (N_ODD, LRU_WIDTH), 0.01),
        'rec_lambda': jnp.log(s_lam) - jnp.log1p(-s_lam),
        'rec_w_out': nrm(ks[13], (N_ODD, LRU_WIDTH, D_MODEL), LRU_WIDTH ** -0.5 * BETA),
        'ln_mix_g': 1.0 + nrm(ks[16], (DEPTH, D_MODEL), 0.01),
        'ln_mix_b': nrm(ks[17], (DEPTH, D_MODEL), 0.01),
        'ln_ffn_g': 1.0 + nrm(ks[18], (DEPTH, D_MODEL), 0.01),
        'ln_ffn_b': nrm(ks[19], (DEPTH, D_MODEL), 0.01),
        'router_group_w': nrm(ks[20], (DEPTH, D_MODEL, N_GROUPS), D_MODEL ** -0.5),
        'router_group_b': nrm(ks[21], (DEPTH, N_GROUPS), 0.01),
        'router_expert_w': nrm(ks[22], (DEPTH, D_MODEL, N_EXPERTS), D_MODEL ** -0.5),
        'router_expert_b': nrm(ks[23], (DEPTH, N_EXPERTS), 0.01),
        'moe_w_gate': nrm(ks[24], (DEPTH, N_EXPERTS, D_MODEL, EXPERT_FF), D_MODEL ** -0.5),
        'moe_w_up': nrm(ks[25], (DEPTH, N_EXPERTS, D_MODEL, EXPERT_FF), D_MODEL ** -0.5),
        'moe_w_down': nrm(ks[26], (DEPTH, N_EXPERTS, EXPERT_FF, D_MODEL), EXPERT_FF ** -0.5 * BETA),
    }


def reference(x, positions, even_w_in, even_w_out, attn_sinks, hgrn_lb_logits, hgrn_gnorm_w,
              rec_w_in, rec_conv_w, rec_conv_b, rec_gate_a_w, rec_gate_a_b, rec_gate_x_w, rec_gate_x_b,
              rec_lambda, rec_w_out, ln_mix_g, ln_mix_b, ln_ffn_g, ln_ffn_b,
              router_group_w, router_group_b, router_expert_w, router_expert_b,
              moe_w_gate, moe_w_up, moe_w_down):
    inv_freq = ROPE_THETA ** (-jnp.arange(0, ROT_DIM, 2, dtype=jnp.float32) / ROT_DIM)
    ang = positions.astype(jnp.float32)[..., None] * inv_freq
    cos = jnp.cos(ang)[:, :, None, :]
    sin = jnp.sin(ang)[:, :, None, :]
    sm = jax.nn.softmax(hgrn_lb_logits.astype(jnp.float32), axis=0)
    lb_table = jnp.cumsum(sm, axis=0) - sm[:1]
    h = x
    for layer in range(DEPTH):
        j = layer // 2
        if layer % 2 == 0:
            mix = even_mixer(h, cos, sin, even_w_in[j], even_w_out[j], attn_sinks[j], lb_table[j], hgrn_gnorm_w[j])
        else:
            mix = recurrent_mixer(h, rec_w_in[j], rec_conv_w[j], rec_conv_b[j], rec_gate_a_w[j], rec_gate_a_b[j],
                                  rec_gate_x_w[j], rec_gate_x_b[j], rec_lambda[j], rec_w_out[j])
        h = layer_norm(ALPHA * h + mix, ln_mix_g[layer], ln_mix_b[layer])
        ffn = hierarchical_moe(h, router_group_w[layer], router_group_b[layer], router_expert_w[layer],
                               router_expert_b[layer], moe_w_gate[layer], moe_w_up[layer], moe_w_down[layer])
        h = layer_norm(ALPHA * h + ffn, ln_ffn_g[layer], ln_ffn_b[layer])
    return h
```

```python
import functools

import jax
import jax.numpy as jnp
from jax import lax
from jax.experimental import pallas as pl
from jax.experimental.pallas import tpu as pltpu

F32 = jnp.float32
BF16 = jnp.bfloat16

D_MODEL = 4096
DEPTH = 4
ATTN_WIDTH = D_MODEL // 2
HEAD_DIM = 64
N_Q_HEADS = ATTN_WIDTH // HEAD_DIM
N_KV_HEADS = N_Q_HEADS // 8
Q_PER_KV = N_Q_HEADS // N_KV_HEADS
KV_WIDTH = N_KV_HEADS * HEAD_DIM
WINDOW = 128
ROT_DIM = HEAD_DIM // 4
ROT_HALF = ROT_DIM // 2
ROPE_THETA = 500000.0
MASK_VALUE = -1e9
HGRN_WIDTH = D_MODEL - ATTN_WIDTH
HGRN_DIM = 128
N_HGRN_HEADS = HGRN_WIDTH // HGRN_DIM
HGRN_CHUNK = 64
HGRN_SUB = 16
LRU_WIDTH = D_MODEL
N_LRU_HEADS = 16
LRU_HEAD_DIM = LRU_WIDTH // N_LRU_HEADS
CONV_WIDTH = 4
RG_C = 8.0
N_GROUPS = 4
EXPERTS_PER_GROUP = 8
N_EXPERTS = N_GROUPS * EXPERTS_PER_GROUP
TOP_K = 2
EXPERT_FF = 3 * D_MODEL // 32
DISPATCH_BLOCK = 128
ALPHA = (2.0 * DEPTH) ** 0.25
LN_EPS = 1e-5
RMS_EPS = 1e-6

LANES = 128
SUBLANES = 8
VMEM_LIMIT = 56 * 1024 * 1024

_NT = (((1,), (1,)), ((), ()))
_TN = (((0,), (0,)), ((), ()))


def _params(*sem):
    return pltpu.CompilerParams(dimension_semantics=sem, vmem_limit_bytes=VMEM_LIMIT)


def _mm_kernel(x_ref, w_ref, o_ref):
    o_ref[...] = jnp.dot(x_ref[...], w_ref[...], preferred_element_type=F32).astype(o_ref.dtype)


def matmul(x, w, out_dtype, tm=1024, tn=512):
    M, K = x.shape
    N = w.shape[1]
    tm = min(tm, M)
    return pl.pallas_call(
        _mm_kernel,
        grid=(M // tm, N // tn),
        in_specs=[pl.BlockSpec((tm, K), lambda i, j: (i, 0)),
                  pl.BlockSpec((K, tn), lambda i, j: (0, j))],
        out_specs=pl.BlockSpec((tm, tn), lambda i, j: (i, j)),
        out_shape=jax.ShapeDtypeStruct((M, N), out_dtype),
        compiler_params=_params("parallel", "arbitrary"),
        name="matmul",
    )(x, w)


def _layer_norm_rows(pre_chunks, g_ref, b_ref, of_ref, ob_ref, width):
    n = len(pre_chunks)
    tn = width // n
    s = None
    for c in range(n):
        t = jnp.sum(pre_chunks[c](), axis=-1, keepdims=True)
        s = t if s is None else s + t
    mu = s * (1.0 / width)
    v = None
    for c in range(n):
        d = pre_chunks[c]() - mu
        t = jnp.sum(d * d, axis=-1, keepdims=True)
        v = t if v is None else v + t
    rstd = lax.rsqrt(v * (1.0 / width) + LN_EPS)
    for c in range(n):
        sl = slice(c * tn, (c + 1) * tn)
        y = (pre_chunks[c]() - mu) * rstd * g_ref[:, sl] + b_ref[:, sl]
        of_ref[:, sl] = y
        ob_ref[:, sl] = y.astype(BF16)


def _mm_ln_kernel(x_ref, w_ref, res_ref, g_ref, b_ref, of_ref, ob_ref, *, nj, tn):
    j = pl.program_id(1)
    pre = ALPHA * res_ref[...] + jnp.dot(x_ref[...], w_ref[...], preferred_element_type=F32)
    for c in range(nj):
        @pl.when(j == c)
        def _(c=c):
            of_ref[:, c * tn:(c + 1) * tn] = pre

    @pl.when(j == nj - 1)
    def _():
        chunks = [functools.partial(lambda c: of_ref[:, c * tn:(c + 1) * tn], c) for c in range(nj)]
        _layer_norm_rows(chunks, g_ref, b_ref, of_ref, ob_ref, nj * tn)


def matmul_residual_ln(x, w, res, g, b, tm=512, tn=512):
    M, K = x.shape
    N = w.shape[1]
    tm = min(tm, M)
    nj = N // tn
    return pl.pallas_call(
        functools.partial(_mm_ln_kernel, nj=nj, tn=tn),
        grid=(M // tm, nj),
        in_specs=[pl.BlockSpec((tm, K), lambda i, j: (i, 0)),
                  pl.BlockSpec((K, tn), lambda i, j: (0, j)),
                  pl.BlockSpec((tm, tn), lambda i, j: (i, j)),
                  pl.BlockSpec((1, N), lambda i, j: (0, 0)),
                  pl.BlockSpec((1, N), lambda i, j: (0, 0))],
        out_specs=[pl.BlockSpec((tm, N), lambda i, j: (i, 0)),
                   pl.BlockSpec((tm, N), lambda i, j: (i, 0))],
        out_shape=[jax.ShapeDtypeStruct((M, N), F32), jax.ShapeDtypeStruct((M, N), BF16)],
        compiler_params=_params("parallel", "arbitrary"),
        name="matmul_residual_ln",
    )(x, w, res, g.reshape(1, N), b.reshape(1, N))


def _rotary(x, tab):
    c, s_lo, s_hi = tab[:, :LANES], tab[:, LANES:2 * LANES], tab[:, 2 * LANES:]
    out = []
    for i in range(x.shape[1] // LANES):
        xc = x[:, i * LANES:(i + 1) * LANES]
        out.append(xc * c + pltpu.roll(xc, LANES - ROT_HALF, 1) * s_lo + pltpu.roll(xc, ROT_HALF, 1) * s_hi)
    return out


def _attn_kernel(sink_ref, q_ref, kc_ref, vc_ref, kp_ref, vp_ref, tc_ref, tp_ref, o_ref):
    n = pl.program_id(1)
    tab_c = tc_ref[...]
    tab_p = tp_ref[...]
    q_cols = _rotary(q_ref[...], tab_c)
    kc = jnp.concatenate(_rotary(kc_ref[...], tab_c), axis=1)
    kp = jnp.concatenate(_rotary(kp_ref[...], tab_p), axis=1)
    k_all = jnp.concatenate([kp, kc], axis=0).astype(BF16)
    v_all = jnp.concatenate([vp_ref[...], vc_ref[...]], axis=0).astype(BF16)
    qi = lax.broadcasted_iota(jnp.int32, (WINDOW, 2 * WINDOW), 0)
    kj = lax.broadcasted_iota(jnp.int32, (WINDOW, 2 * WINDOW), 1)
    rel = qi + WINDOW - kj
    mask = (rel >= 0) & (rel < WINDOW) & ((kj >= WINDOW) | (n > 0))
    outs = []
    for h in range(N_Q_HEADS):
        kv = h // Q_PER_KV
        qh = q_cols[h // 2][:, (h % 2) * HEAD_DIM:(h % 2 + 1) * HEAD_DIM].astype(BF16)
        kh = k_all[:, kv * HEAD_DIM:(kv + 1) * HEAD_DIM]
        vh = v_all[:, kv * HEAD_DIM:(kv + 1) * HEAD_DIM]
        s = lax.dot_general(qh, kh, _NT, preferred_element_type=F32) * (HEAD_DIM ** -0.5)
        s = jnp.where(mask, s, MASK_VALUE)
        sink = sink_ref[h]
        m = jnp.maximum(jnp.max(s, axis=-1, keepdims=True), sink)
        p = jnp.where(mask, jnp.exp(s - m), 0.0)
        denom = jnp.sum(p, axis=-1, keepdims=True) + jnp.exp(sink - m)
        o = jnp.dot(p.astype(BF16), vh, preferred_element_type=F32) / denom
        outs.append(o)
    o_ref[...] = jnp.concatenate(outs, axis=1).astype(o_ref.dtype)


def swa_attention(proj, tab, sinks, batch, seq):
    nb = seq // WINDOW
    kcol = ATTN_WIDTH // KV_WIDTH
    cur = lambda b, n: b * nb + n
    prev = lambda b, n: b * nb + jnp.maximum(n - 1, 0)
    return pl.pallas_call(
        _attn_kernel,
        grid=(batch, nb),
        in_specs=[pl.BlockSpec(memory_space=pltpu.SMEM),
                  pl.BlockSpec((WINDOW, ATTN_WIDTH), lambda b, n: (cur(b, n), 0)),
                  pl.BlockSpec((WINDOW, KV_WIDTH), lambda b, n: (cur(b, n), kcol)),
                  pl.BlockSpec((WINDOW, KV_WIDTH), lambda b, n: (cur(b, n), kcol + 1)),
                  pl.BlockSpec((WINDOW, KV_WIDTH), lambda b, n: (prev(b, n), kcol)),
                  pl.BlockSpec((WINDOW, KV_WIDTH), lambda b, n: (prev(b, n), kcol + 1)),
                  pl.BlockSpec((WINDOW, 3 * LANES), lambda b, n: (cur(b, n), 0)),
                  pl.BlockSpec((WINDOW, 3 * LANES), lambda b, n: (prev(b, n), 0))],
        out_specs=pl.BlockSpec((WINDOW, ATTN_WIDTH), lambda b, n: (cur(b, n), 0)),
        out_shape=jax.ShapeDtypeStruct((batch * seq, ATTN_WIDTH), BF16),
        compiler_params=_params("parallel", "parallel"),
        name="swa_attention",
    )(sinks, proj, proj, proj, proj, proj, tab, tab)


def _hgrn_chunk(q_in, z, v, g_in, lb, gn, state_ref):
    C, c = HGRN_CHUNK, HGRN_SUB
    sg = jax.nn.sigmoid(z)
    lf = jnp.log(lb + (1.0 - lb) * sg)
    k = (1.0 - lb) * jax.nn.sigmoid(-z)
    q = q_in * jax.nn.sigmoid(q_in)
    ti = lax.broadcasted_iota(jnp.int32, (C, C), 0)
    si = lax.broadcasted_iota(jnp.int32, (C, C), 1)
    tril = jnp.where(ti >= si, 1.0, 0.0).astype(F32)
    b = jnp.dot(tril, lf, preferred_element_type=F32, precision=lax.Precision.HIGHEST)
    b_last = b[C - 1:C, :]
    state = state_ref[...]
    o_inter = lax.dot_general((q * jnp.exp(b)).astype(BF16), state.astype(BF16), _NT, preferred_element_type=F32)
    rows = lax.broadcasted_iota(jnp.int32, (c, HGRN_DIM), 0)
    pieces = [o_inter[j * c:(j + 1) * c] for j in range(C // c)]
    for j in range(C // c):
        lo, hi = j * c, (j + 1) * c
        bd, qd, kd, vd = b[lo:hi], q[lo:hi], k[lo:hi], v[lo:hi]
        e_j = b[hi - 1:hi, :]
        if hi < C:
            khat = (kd * jnp.exp(e_j - bd)).astype(BF16)
            qt = (q[hi:] * jnp.exp(b[hi:] - e_j)).astype(BF16)
            sc = lax.dot_general(qt, khat, _NT, preferred_element_type=F32)
            upd = jnp.dot(sc.astype(BF16), vd.astype(BF16), preferred_element_type=F32)
            for jj in range(j + 1, C // c):
                pieces[jj] = pieces[jj] + upd[(jj - j - 1) * c:(jj - j) * c]
        acc = jnp.zeros((c, HGRN_DIM), F32)
        for s in range(c):
            keep = rows >= s
            dec = jnp.where(keep, jnp.exp(jnp.where(keep, bd - bd[s:s + 1], 0.0)), 0.0)
            col = jnp.sum(qd * kd[s:s + 1] * dec, axis=-1, keepdims=True)
            acc = acc + col * vd[s:s + 1]
        pieces[j] = pieces[j] + acc
    o = jnp.concatenate(pieces, axis=0)
    kdec = (k * jnp.exp(b_last - b)).astype(BF16)
    state_ref[...] = state * jnp.exp(b_last) + lax.dot_general(v.astype(BF16), kdec, _TN, preferred_element_type=F32)
    ms = jnp.mean(o * o, axis=-1, keepdims=True)
    return o * lax.rsqrt(ms + RMS_EPS) * gn * (g_in * jax.nn.sigmoid(g_in))


def _hgrn_kernel(q_ref, f_ref, i_ref, g_ref, lb_ref, gn_ref, o_ref, state_ref, *, n_chunks):
    @pl.when(pl.program_id(2) == 0)
    def _():
        state_ref[...] = jnp.zeros_like(state_ref)

    lb = lb_ref[...]
    gn = gn_ref[...]

    def body(ci, carry):
        r0 = pl.multiple_of(ci * HGRN_CHUNK, HGRN_CHUNK)
        sl = pl.ds(r0, HGRN_CHUNK)
        y = _hgrn_chunk(q_ref[sl, :], f_ref[sl, :], i_ref[sl, :], g_ref[sl, :], lb, gn, state_ref)
        o_ref[sl, :] = y.astype(o_ref.dtype)
        return carry

    lax.fori_loop(0, n_chunks, body, 0)


def hgrn2(proj, lb, gnorm, batch, seq, tile=512):
    tile = min(tile, seq)
    nt = seq // tile
    c0 = (ATTN_WIDTH + 2 * KV_WIDTH) // HGRN_DIM
    step = HGRN_WIDTH // HGRN_DIM
    spec = lambda k: pl.BlockSpec((tile, HGRN_DIM), lambda b, h, t: (b * nt + t, c0 + k * step + h))
    return pl.pallas_call(
        functools.partial(_hgrn_kernel, n_chunks=tile // HGRN_CHUNK),
        grid=(batch, N_HGRN_HEADS, nt),
        in_specs=[spec(0), spec(1), spec(2), spec(3),
                  pl.BlockSpec((1, HGRN_DIM), lambda b, h, t: (0, h)),
                  pl.BlockSpec((1, HGRN_DIM), lambda b, h, t: (0, 0))],
        out_specs=pl.BlockSpec((tile, HGRN_DIM), lambda b, h, t: (b * nt + t, h)),
        out_shape=jax.ShapeDtypeStruct((batch * seq, HGRN_WIDTH), BF16),
        scratch_shapes=[pltpu.VMEM((HGRN_DIM, HGRN_DIM), F32)],
        compiler_params=_params("parallel", "parallel", "arbitrary"),
        name="hgrn2",
    )(proj, proj, proj, proj, lb.reshape(1, HGRN_WIDTH), gnorm.reshape(1, HGRN_DIM))


def _lru_kernel(gate_ref, rnn_ref, cw_ref, cb_ref, wa_ref, wx_ref, ba_ref, bx_ref, lam_ref, o_ref,
                xs_ref, h_ref, *, tile):
    @pl.when(pl.program_id(2) == 0)
    def _():
        xs_ref[0:SUBLANES, :] = jnp.zeros((SUBLANES, LRU_HEAD_DIM), F32)
        h_ref[...] = jnp.zeros_like(h_ref)

    x = rnn_ref[...]
    xs_ref[SUBLANES:, :] = x
    u = cb_ref[...] + x * cw_ref[CONV_WIDTH - 1:CONV_WIDTH, :]
    for j in range(CONV_WIDTH - 1):
        back = CONV_WIDTH - 1 - j
        u = u + xs_ref[pl.ds(SUBLANES - back, tile), :] * cw_ref[j:j + 1, :]
    xs_ref[0:SUBLANES, :] = x[tile - SUBLANES:, :]

    ub = u.astype(BF16)
    r = jax.nn.sigmoid(jnp.dot(ub, wa_ref[...], preferred_element_type=F32) + ba_ref[...])
    ig = jax.nn.sigmoid(jnp.dot(ub, wx_ref[...], preferred_element_type=F32) + bx_ref[...])
    nl = -lam_ref[...]
    softplus = jnp.maximum(nl, 0.0) + jnp.log1p(jnp.exp(-jnp.abs(nl)))
    log_a = -RG_C * r * softplus
    a = jnp.exp(log_a)
    bb = jnp.sqrt(jnp.maximum(-jnp.tanh(log_a) * (a * a + 1.0), 0.0)) * (ig * u)

    rows = lax.broadcasted_iota(jnp.int32, (tile, LRU_HEAD_DIM), 0)
    k = 1
    while k < tile:
        keep = rows >= k
        a_sh = jnp.where(keep, pltpu.roll(a, k, 0), 1.0)
        b_sh = jnp.where(keep, pltpu.roll(bb, k, 0), 0.0)
        bb = a * b_sh + bb
        a = a * a_sh
        k *= 2
    h = a * h_ref[...] + bb
    h_ref[...] = h[tile - 1:tile, :]
    o_ref[...] = (h * jax.nn.gelu(gate_ref[...])).astype(o_ref.dtype)


def recurrent_block(proj, conv_w, conv_b, wa, ba, wx, bx, lam, batch, seq, tile=256):
    tile = min(tile, seq)
    nt = seq // tile
    hd = LRU_HEAD_DIM
    row = lambda arr: arr.reshape(1, LRU_WIDTH)
    vec = pl.BlockSpec((1, hd), lambda b, h, t: (0, h))
    return pl.pallas_call(
        functools.partial(_lru_kernel, tile=tile),
        grid=(batch, N_LRU_HEADS, nt),
        in_specs=[pl.BlockSpec((tile, hd), lambda b, h, t: (b * nt + t, h)),
                  pl.BlockSpec((tile, hd), lambda b, h, t: (b * nt + t, N_LRU_HEADS + h)),
                  pl.BlockSpec((CONV_WIDTH, hd), lambda b, h, t: (0, h)),
                  vec,
                  pl.BlockSpec((None, hd, hd), lambda b, h, t: (h, 0, 0)),
                  pl.BlockSpec((None, hd, hd), lambda b, h, t: (h, 0, 0)),
                  vec, vec, vec],
        out_specs=pl.BlockSpec((tile, hd), lambda b, h, t: (b * nt + t, h)),
        out_shape=jax.ShapeDtypeStruct((batch * seq, LRU_WIDTH), BF16),
        scratch_shapes=[pltpu.VMEM((tile + SUBLANES, hd), F32), pltpu.VMEM((1, hd), F32)],
        compiler_params=_params("parallel", "parallel", "arbitrary"),
        name="recurrent_block",
    )(proj, proj, conv_w, row(conv_b), wa, wx, row(ba), row(bx), row(lam))


def _router_kernel(h_ref, w_ref, b_ref, idx_ref, wt_ref):
    logits = jnp.dot(h_ref[...], w_ref[...], preferred_element_type=F32,
                     precision=lax.Precision.HIGHEST) + b_ref[...]
    lane = lax.broadcasted_iota(jnp.int32, logits.shape, 1)
    neg = -jnp.inf
    big = jnp.int32(LANES)
    gl = jnp.where(lane < N_GROUPS, logits, neg)
    gmax = jnp.max(gl, axis=-1, keepdims=True)
    g_sel = jnp.min(jnp.where(gl == gmax, lane, big), axis=-1, keepdims=True)
    p_top = 1.0 / jnp.sum(jnp.exp(gl - gmax), axis=-1, keepdims=True)
    lo = N_GROUPS + g_sel * EXPERTS_PER_GROUP
    el = jnp.where((lane >= lo) & (lane < lo + EXPERTS_PER_GROUP), logits, neg)
    m1 = jnp.max(el, axis=-1, keepdims=True)
    i1 = jnp.min(jnp.where(el == m1, lane, big), axis=-1, keepdims=True)
    el2 = jnp.where(lane == i1, neg, el)
    m2 = jnp.max(el2, axis=-1, keepdims=True)
    i2 = jnp.min(jnp.where(el2 == m2, lane, big), axis=-1, keepdims=True)
    z = jnp.sum(jnp.exp(el - m1), axis=-1, keepdims=True)
    p1 = 1.0 / z
    p2 = jnp.exp(m2 - m1) / z
    tot = p1 + p2
    idx_ref[...] = jnp.where(lane == 0, i1 - N_GROUPS, jnp.where(lane == 1, i2 - N_GROUPS, 0))
    wt_ref[...] = jnp.where(lane == 0, p1 / tot * p_top, jnp.where(lane == 1, p2 / tot * p_top, 0.0))


def moe_router(h, w, b, tm=512):
    N, D = h.shape
    tm = min(tm, N)
    return pl.pallas_call(
        _router_kernel,
        grid=(N // tm,),
        in_specs=[pl.BlockSpec((tm, D), lambda i: (i, 0)),
                  pl.BlockSpec((D, LANES), lambda i: (0, 0)),
                  pl.BlockSpec((1, LANES), lambda i: (0, 0))],
        out_specs=[pl.BlockSpec((tm, LANES), lambda i: (i, 0)),
                   pl.BlockSpec((tm, LANES), lambda i: (i, 0))],
        out_shape=[jax.ShapeDtypeStruct((N, LANES), jnp.int32), jax.ShapeDtypeStruct((N, LANES), F32)],
        compiler_params=_params("parallel"),
        name="moe_router",
    )(h, w, b)


def _expert_kernel(be_ref, tok_ref, nused_ref, h_hbm, wgu_ref, wd_ref, y_ref, xbuf, sem):
    blk = pl.program_id(0)

    @pl.when(blk < nused_ref[0])
    def _():
        base = blk * DISPATCH_BLOCK

        def issue(r, carry):
            tok = tok_ref[base + r]
            pltpu.make_async_copy(h_hbm.at[pl.ds(tok, 1)], xbuf.at[pl.ds(r, 1)], sem).start()
            return carry

        lax.fori_loop(0, DISPATCH_BLOCK, issue, 0, unroll=8)
        pltpu.make_async_copy(h_hbm.at[pl.ds(0, DISPATCH_BLOCK)], xbuf, sem).wait()
        xb = xbuf[...].astype(BF16)
        gu = jnp.dot(xb, wgu_ref[...], preferred_element_type=F32)
        gate, up = gu[:, :EXPERT_FF], gu[:, EXPERT_FF:]
        hid = (gate * jax.nn.sigmoid(gate) * up).astype(BF16)
        y_ref[...] = jnp.dot(hid, wd_ref[...], preferred_element_type=F32)

    @pl.when(blk >= nused_ref[0])
    def _():
        y_ref[...] = jnp.zeros_like(y_ref)


def moe_experts(h, wgu, wd, block_e, slot_tok, n_used):
    N, D = h.shape
    n_blocks = block_e.shape[0]
    grid_spec = pltpu.PrefetchScalarGridSpec(
        num_scalar_prefetch=3,
        grid=(n_blocks,),
        in_specs=[pl.BlockSpec(memory_space=pl.ANY),
                  pl.BlockSpec((None, D, 2 * EXPERT_FF), lambda i, be, tok, nu: (be[i], 0, 0)),
                  pl.BlockSpec((None, EXPERT_FF, D), lambda i, be, tok, nu: (be[i], 0, 0))],
        out_specs=pl.BlockSpec((DISPATCH_BLOCK, D), lambda i, be, tok, nu: (i, 0)),
        scratch_shapes=[pltpu.VMEM((DISPATCH_BLOCK, D), F32), pltpu.SemaphoreType.DMA(())],
    )
    return pl.pallas_call(
        _expert_kernel,
        grid_spec=grid_spec,
        out_shape=jax.ShapeDtypeStruct((n_blocks * DISPATCH_BLOCK, D), F32),
        compiler_params=_params("arbitrary"),
        name="moe_experts",
    )(block_e, slot_tok, n_used, h, wgu, wd)


def _combine_ln_kernel(dest_ref, y_hbm, h_ref, wt_ref, g_ref, b_ref, of_ref, ob_ref, ybuf, sem, *, tm, nchunk):
    base = pl.program_id(0) * (tm * TOP_K)

    def issue(r, carry):
        for k in range(TOP_K):
            d = dest_ref[base + r * TOP_K + k]
            pltpu.make_async_copy(y_hbm.at[pl.ds(d, 1)], ybuf.at[k, pl.ds(r, 1)], sem).start()
        return carry

    lax.fori_loop(0, tm, issue, 0, unroll=8)
    for k in range(TOP_K):
        pltpu.make_async_copy(y_hbm.at[pl.ds(0, tm)], ybuf.at[k], sem).wait()
    D = h_ref.shape[1]
    tn = D // nchunk
    w0 = wt_ref[:, 0:1]
    w1 = wt_ref[:, 1:2]

    def chunk(c):
        sl = slice(c * tn, (c + 1) * tn)
        return ALPHA * h_ref[:, sl] + (ybuf[0, :, sl] * w0 + ybuf[1, :, sl] * w1)

    _layer_norm_rows([functools.partial(chunk, c) for c in range(nchunk)], g_ref, b_ref, of_ref, ob_ref, D)


def moe_combine_ln(y, h, wts, dest, g, b, tm=256, nchunk=8):
    N, D = h.shape
    tm = min(tm, N)
    grid_spec = pltpu.PrefetchScalarGridSpec(
        num_scalar_prefetch=1,
        grid=(N // tm,),
        in_specs=[pl.BlockSpec(memory_space=pl.ANY),
                  pl.BlockSpec((tm, D), lambda i, d: (i, 0)),
                  pl.BlockSpec((tm, LANES), lambda i, d: (i, 0)),
                  pl.BlockSpec((1, D), lambda i, d: (0, 0)),
                  pl.BlockSpec((1, D), lambda i, d: (0, 0))],
        out_specs=[pl.BlockSpec((tm, D), lambda i, d: (i, 0)),
                   pl.BlockSpec((tm, D), lambda i, d: (i, 0))],
        scratch_shapes=[pltpu.VMEM((TOP_K, tm, D), F32), pltpu.SemaphoreType.DMA(())],
    )
    return pl.pallas_call(
        functools.partial(_combine_ln_kernel, tm=tm, nchunk=nchunk),
        grid_spec=grid_spec,
        out_shape=[jax.ShapeDtypeStruct((N, D), F32), jax.ShapeDtypeStruct((N, D), BF16)],
        compiler_params=_params("arbitrary"),
        name="moe_combine_ln",
    )(dest, y, h, wts, g.reshape(1, D), b.reshape(1, D))


def _dispatch_plan(e_idx):
    N = e_idx.shape[0]
    A = N * TOP_K
    n_blocks = (A + N_EXPERTS * (DISPATCH_BLOCK - 1) + DISPATCH_BLOCK - 1) // DISPATCH_BLOCK
    flat_e = e_idx.reshape(-1)
    onehot = jax.nn.one_hot(flat_e, N_EXPERTS, dtype=jnp.int32)
    counts = jnp.sum(onehot, axis=0)
    rank = jnp.sum((jnp.cumsum(onehot, axis=0) - 1) * onehot, axis=1)
    padded = (counts + DISPATCH_BLOCK - 1) // DISPATCH_BLOCK * DISPATCH_BLOCK
    pad_end = jnp.cumsum(padded)
    dest = (pad_end[flat_e] - padded[flat_e] + rank).astype(jnp.int32)
    slot_tok = jnp.zeros((n_blocks * DISPATCH_BLOCK,), jnp.int32).at[dest].set(
        jnp.arange(A, dtype=jnp.int32) // TOP_K)
    block_e = jnp.minimum(
        jnp.searchsorted(pad_end, jnp.arange(n_blocks) * DISPATCH_BLOCK, side='right'), N_EXPERTS - 1)
    n_used = (pad_end[-1] // DISPATCH_BLOCK).astype(jnp.int32).reshape(1)
    return block_e.astype(jnp.int32), slot_tok, n_used, dest


def _moe_ln(h, router_w, router_b, wgu, wd, g, b):
    idx, wts = moe_router(h, router_w, router_b)
    block_e, slot_tok, n_used, dest = _dispatch_plan(idx[:, :TOP_K])
    y = moe_experts(h, wgu, wd, block_e, slot_tok, n_used)
    return moe_combine_ln(y, h, wts, dest, g, b)


def _rotary_table(positions):
    inv_freq = ROPE_THETA ** (-jnp.arange(0, ROT_DIM, 2, dtype=F32) / ROT_DIM)
    ang = positions.astype(F32).reshape(-1)[:, None] * inv_freq
    cos, sin = jnp.cos(ang), jnp.sin(ang)
    n = ang.shape[0]
    rest = HEAD_DIM - ROT_DIM
    z = lambda w: jnp.zeros((n, w), F32)
    c = jnp.concatenate([cos, cos, jnp.ones((n, rest), F32)], axis=1)
    s_lo = jnp.concatenate([-sin, z(HEAD_DIM - ROT_HALF)], axis=1)
    s_hi = jnp.concatenate([z(ROT_HALF), sin, z(rest)], axis=1)
    rep = LANES // HEAD_DIM
    return jnp.concatenate([jnp.tile(c, (1, rep)), jnp.tile(s_lo, (1, rep)), jnp.tile(s_hi, (1, rep))], axis=1)


def kernel(x, positions, even_w_in, even_w_out, attn_sinks, hgrn_lb_logits, hgrn_gnorm_w, rec_w_in, rec_conv_w, rec_conv_b, rec_gate_a_w, rec_gate_a_b, rec_gate_x_w, rec_gate_x_b, rec_lambda, rec_w_out, ln_mix_g, ln_mix_b, ln_ffn_g, ln_ffn_b, router_group_w, router_group_b, router_expert_w, router_expert_b, moe_w_gate, moe_w_up, moe_w_down):
    B, S, D = x.shape
    N = B * S
    tab = _rotary_table(positions)
    sm = jax.nn.softmax(hgrn_lb_logits.astype(F32), axis=0)
    lb_table = jnp.cumsum(sm, axis=0) - sm[:1]
    pad = LANES - N_GROUPS - N_EXPERTS
    router_w = jnp.concatenate([router_group_w, router_expert_w, jnp.zeros((DEPTH, D, pad), F32)], axis=-1)
    router_b = jnp.concatenate([router_group_b, router_expert_b, jnp.zeros((DEPTH, pad), F32)], axis=-1)

    h = x.reshape(N, D)
    hb = h.astype(BF16)
    for layer in range(DEPTH):
        j = layer // 2
        if layer % 2 == 0:
            proj = matmul(hb, even_w_in[j].astype(BF16), F32)
            o_a = swa_attention(proj, tab, attn_sinks[j], B, S)
            o_b = hgrn2(proj, lb_table[j], hgrn_gnorm_w[j], B, S)
            mixed = jnp.concatenate([o_a, o_b], axis=1)
            w_out = even_w_out[j]
        else:
            proj = matmul(hb, rec_w_in[j].astype(BF16), F32)
            mixed = recurrent_block(proj, rec_conv_w[j], rec_conv_b[j], rec_gate_a_w[j].astype(BF16),
                                    rec_gate_a_b[j], rec_gate_x_w[j].astype(BF16), rec_gate_x_b[j],
                                    rec_lambda[j], B, S)
            w_out = rec_w_out[j]
        h, hb = matmul_residual_ln(mixed, w_out.astype(BF16), h, ln_mix_g[layer], ln_mix_b[layer])
        wgu = jnp.concatenate([moe_w_gate[layer], moe_w_up[layer]], axis=-1).astype(BF16)
        h, hb = _moe_ln(h, router_w[layer], router_b[layer].reshape(1, LANES), wgu,
                        moe_w_down[layer].astype(BF16), ln_ffn_g[layer], ln_ffn_b[layer])
    return h.reshape(B, S, D)
```

```python
import functools

import jax
import jax.numpy as jnp
from jax import lax
from jax.experimental import pallas as pl
from jax.experimental.pallas import tpu as pltpu

F32 = jnp.float32
BF16 = jnp.bfloat16

D_MODEL = 4096
DEPTH = 4
ATTN_WIDTH = D_MODEL // 2
HEAD_DIM = 64
N_Q_HEADS = ATTN_WIDTH // HEAD_DIM
N_KV_HEADS = N_Q_HEADS // 8
Q_PER_KV = N_Q_HEADS // N_KV_HEADS
KV_WIDTH = N_KV_HEADS * HEAD_DIM
WINDOW = 128
ROT_DIM = HEAD_DIM // 4
ROT_HALF = ROT_DIM // 2
ROPE_THETA = 500000.0
MASK_VALUE = -1e9
HGRN_WIDTH = D_MODEL - ATTN_WIDTH
HGRN_DIM = 128
N_HGRN_HEADS = HGRN_WIDTH // HGRN_DIM
HGRN_CHUNK = 64
HGRN_SUB = 16
HGRN_HEADS_PER_STEP = 4
HGRN_PIVOT_MAX_DECAY = 75.0
LRU_WIDTH = D_MODEL
N_LRU_HEADS = 16
LRU_HEAD_DIM = LRU_WIDTH // N_LRU_HEADS
CONV_WIDTH = 4
RG_C = 8.0
N_GROUPS = 4
EXPERTS_PER_GROUP = 8
N_EXPERTS = N_GROUPS * EXPERTS_PER_GROUP
TOP_K = 2
EXPERT_FF = 3 * D_MODEL // 32
EXPERT_TILE = 256
ALPHA = (2.0 * DEPTH) ** 0.25
LN_EPS = 1e-5
RMS_EPS = 1e-6

LANES = 128
SUBLANES = 8
VMEM_LIMIT = 56 * 1024 * 1024
ROW_CHUNKS = D_MODEL // LANES
NEG_BIG = -1e30

_NT = (((1,), (1,)), ((), ()))
_TN = (((0,), (0,)), ((), ()))


def _params(*sem):
    return pltpu.CompilerParams(dimension_semantics=sem, vmem_limit_bytes=VMEM_LIMIT)


def _token_major(ref, c, rows):
    return ref.at[pl.ds(c, rows, stride=ROW_CHUNKS), :]


def _mm_kernel(x_ref, w_ref, o_ref):
    o_ref[...] = jnp.dot(x_ref[...], w_ref[...], preferred_element_type=F32).astype(o_ref.dtype)


def matmul(x, w, layer, out_dtype, tm=1024, tn=512):
    M, K = x.shape
    N = w.shape[2]
    tm = min(tm, M)
    return pl.pallas_call(
        _mm_kernel,
        grid=(M // tm, N // tn),
        in_specs=[pl.BlockSpec((tm, K), lambda i, j: (i, 0)),
                  pl.BlockSpec((None, K, tn), lambda i, j: (layer, 0, j))],
        out_specs=pl.BlockSpec((tm, tn), lambda i, j: (i, j)),
        out_shape=jax.ShapeDtypeStruct((M, N), out_dtype),
        compiler_params=_params("parallel", "arbitrary"),
        name="matmul",
    )(x, w)


def _ln_stats(chunk, n):
    acc = chunk(0)
    for c in range(1, n):
        acc = acc + chunk(c)
    mu = jnp.sum(acc, axis=-1, keepdims=True) * (1.0 / (n * LANES))
    acc = None
    for c in range(n):
        d = chunk(c) - mu
        acc = d * d if acc is None else acc + d * d
    var = jnp.sum(acc, axis=-1, keepdims=True) * (1.0 / (n * LANES))
    return mu, lax.rsqrt(var + LN_EPS)


def _route(h, w_ref, b_ref, idx_ref, wt_ref):
    logits = jnp.dot(h, w_ref[...], preferred_element_type=F32, precision=lax.Precision.HIGHEST) + b_ref[...]
    lane = lax.broadcasted_iota(jnp.int32, logits.shape, 1)
    neg = -jnp.inf
    big = jnp.int32(LANES)
    gl = jnp.where(lane < N_GROUPS, logits, neg)
    gmax = jnp.max(gl, axis=-1, keepdims=True)
    g_sel = jnp.min(jnp.where(gl == gmax, lane, big), axis=-1, keepdims=True)
    p_top = 1.0 / jnp.sum(jnp.exp(gl - gmax), axis=-1, keepdims=True)
    lo = N_GROUPS + g_sel * EXPERTS_PER_GROUP
    el = jnp.where((lane >= lo) & (lane < lo + EXPERTS_PER_GROUP), logits, neg)
    m1 = jnp.max(el, axis=-1, keepdims=True)
    i1 = jnp.min(jnp.where(el == m1, lane, big), axis=-1, keepdims=True)
    el2 = jnp.where(lane == i1, neg, el)
    m2 = jnp.max(el2, axis=-1, keepdims=True)
    i2 = jnp.min(jnp.where(el2 == m2, lane, big), axis=-1, keepdims=True)
    z = jnp.sum(jnp.exp(el - m1), axis=-1, keepdims=True)
    p1 = 1.0 / z
    p2 = jnp.exp(m2 - m1) / z
    tot = p1 + p2
    idx_ref[...] = jnp.where(lane == 0, i1 - N_GROUPS, jnp.where(lane == 1, i2 - N_GROUPS, 0))
    wt_ref[...] = jnp.where(lane == 0, p1 / tot * p_top, jnp.where(lane == 1, p2 / tot * p_top, 0.0))


def _mm_ln_route_kernel(x_ref, w_ref, res_ref, g_ref, b_ref, rw_ref, rb_ref, ht_ref, idx_ref, wt_ref, pre_ref,
                        *, nj, tn, tm):
    j = pl.program_id(1)
    pre = ALPHA * res_ref[...] + jnp.dot(x_ref[...], w_ref[...], preferred_element_type=F32)
    for c in range(nj):
        @pl.when(j == c)
        def _(c=c):
            pre_ref[:, c * tn:(c + 1) * tn] = pre

    @pl.when(j == nj - 1)
    def _():
        n = (nj * tn) // LANES
        mu, rstd = _ln_stats(lambda c: pre_ref[:, c * LANES:(c + 1) * LANES], n)
        for c in range(n):
            sl = slice(c * LANES, (c + 1) * LANES)
            y = (pre_ref[:, sl] - mu) * rstd * g_ref[:, sl] + b_ref[:, sl]
            pre_ref[:, sl] = y
            _token_major(ht_ref, c, tm)[...] = y
        _route(pre_ref[...], rw_ref, rb_ref, idx_ref, wt_ref)


def matmul_residual_ln_route(x, w, res, g, b, rw, rb, layer, tm=512, tn=512):
    M, K = x.shape
    N = w.shape[2]
    tm = min(tm, M)
    nj = N // tn
    vec = pl.BlockSpec((None, 1, N), lambda i, j: (layer, 0, 0))
    return pl.pallas_call(
        functools.partial(_mm_ln_route_kernel, nj=nj, tn=tn, tm=tm),
        grid=(M // tm, nj),
        in_specs=[pl.BlockSpec((tm, K), lambda i, j: (i, 0)),
                  pl.BlockSpec((None, K, tn), lambda i, j: (layer // 2, 0, j)),
                  pl.BlockSpec((tm, tn), lambda i, j: (i, j)),
                  vec, vec,
                  pl.BlockSpec((None, N, LANES), lambda i, j: (layer, 0, 0)),
                  pl.BlockSpec((None, 1, LANES), lambda i, j: (layer, 0, 0))],
        out_specs=[pl.BlockSpec((tm * ROW_CHUNKS, LANES), lambda i, j: (i, 0)),
                   pl.BlockSpec((tm, LANES), lambda i, j: (i, 0)),
                   pl.BlockSpec((tm, LANES), lambda i, j: (i, 0))],
        out_shape=[jax.ShapeDtypeStruct((M * ROW_CHUNKS, LANES), F32),
                   jax.ShapeDtypeStruct((M, LANES), jnp.int32),
                   jax.ShapeDtypeStruct((M, LANES), F32)],
        scratch_shapes=[pltpu.VMEM((tm, N), F32)],
        compiler_params=_params("parallel", "arbitrary"),
        name="matmul_residual_ln_route",
    )(x, w, res, g, b, rw, rb)


def _rotary(x, tab):
    c, s_lo, s_hi = tab[:, :LANES], tab[:, LANES:2 * LANES], tab[:, 2 * LANES:]
    out = []
    for i in range(x.shape[1] // LANES):
        xc = x[:, i * LANES:(i + 1) * LANES]
        out.append(xc * c + pltpu.roll(xc, LANES - ROT_HALF, 1) * s_lo + pltpu.roll(xc, ROT_HALF, 1) * s_hi)
    return out


def _attn_kernel(sink_ref, q_ref, kc_ref, vc_ref, kp_ref, vp_ref, tc_ref, tp_ref, o_ref):
    n = pl.program_id(1)
    tab_c = tc_ref[...]
    tab_p = tp_ref[...]
    q_cols = _rotary(q_ref[...], tab_c)
    kc = jnp.concatenate(_rotary(kc_ref[...], tab_c), axis=1)
    kp = jnp.concatenate(_rotary(kp_ref[...], tab_p), axis=1)
    k_all = jnp.concatenate([kp, kc], axis=0).astype(BF16)
    v_all = jnp.concatenate([vp_ref[...], vc_ref[...]], axis=0).astype(BF16)
    qi = lax.broadcasted_iota(jnp.int32, (WINDOW, 2 * WINDOW), 0)
    kj = lax.broadcasted_iota(jnp.int32, (WINDOW, 2 * WINDOW), 1)
    rel = qi + WINDOW - kj
    mask = (rel >= 0) & (rel < WINDOW) & ((kj >= WINDOW) | (n > 0))
    outs = []
    for h in range(N_Q_HEADS):
        kv = h // Q_PER_KV
        qh = q_cols[h // 2][:, (h % 2) * HEAD_DIM:(h % 2 + 1) * HEAD_DIM].astype(BF16)
        kh = k_all[:, kv * HEAD_DIM:(kv + 1) * HEAD_DIM]
        vh = v_all[:, kv * HEAD_DIM:(kv + 1) * HEAD_DIM]
        s = lax.dot_general(qh, kh, _NT, preferred_element_type=F32) * (HEAD_DIM ** -0.5)
        s = jnp.where(mask, s, MASK_VALUE)
        sink = sink_ref[h]
        m = jnp.maximum(jnp.max(s, axis=-1, keepdims=True), sink)
        p = jnp.where(mask, jnp.exp(s - m), 0.0)
        denom = jnp.sum(p, axis=-1, keepdims=True) + jnp.exp(sink - m)
        o = jnp.dot(p.astype(BF16), vh, preferred_element_type=F32) / denom
        outs.append(o)
    o_ref[...] = jnp.concatenate(outs, axis=1).astype(o_ref.dtype)


def swa_attention(proj, tab, sinks, batch, seq):
    nb = seq // WINDOW
    kcol = ATTN_WIDTH // KV_WIDTH
    cur = lambda b, n: b * nb + n
    prev = lambda b, n: b * nb + jnp.maximum(n - 1, 0)
    return pl.pallas_call(
        _attn_kernel,
        grid=(batch, nb),
        in_specs=[pl.BlockSpec(memory_space=pltpu.SMEM),
                  pl.BlockSpec((WINDOW, ATTN_WIDTH), lambda b, n: (cur(b, n), 0)),
                  pl.BlockSpec((WINDOW, KV_WIDTH), lambda b, n: (cur(b, n), kcol)),
                  pl.BlockSpec((WINDOW, KV_WIDTH), lambda b, n: (cur(b, n), kcol + 1)),
                  pl.BlockSpec((WINDOW, KV_WIDTH), lambda b, n: (prev(b, n), kcol)),
                  pl.BlockSpec((WINDOW, KV_WIDTH), lambda b, n: (prev(b, n), kcol + 1)),
                  pl.BlockSpec((WINDOW, 3 * LANES), lambda b, n: (cur(b, n), 0)),
                  pl.BlockSpec((WINDOW, 3 * LANES), lambda b, n: (prev(b, n), 0))],
        out_specs=pl.BlockSpec((WINDOW, ATTN_WIDTH), lambda b, n: (cur(b, n), 0)),
        out_shape=jax.ShapeDtypeStruct((batch * seq, ATTN_WIDTH), BF16),
        compiler_params=_params("parallel", "parallel"),
        name="swa_attention",
    )(sinks, proj, proj, proj, proj, proj, tab, tab)


def _hgrn_gates(q_in, z, lb):
    C = HGRN_CHUNK
    sg = jax.nn.sigmoid(z)
    lf = jnp.log(lb + (1.0 - lb) * sg)
    k = (1.0 - lb) * jax.nn.sigmoid(-z)
    q = q_in * jax.nn.sigmoid(q_in)
    ti = lax.broadcasted_iota(jnp.int32, (C, C), 0)
    si = lax.broadcasted_iota(jnp.int32, (C, C), 1)
    tril = jnp.where(ti >= si, 1.0, 0.0).astype(F32)
    b = jnp.dot(tril, lf, preferred_element_type=F32, precision=lax.Precision.HIGHEST)
    return q, k, b


def _hgrn_intra_pivot(q, k, v, b):
    C = HGRN_CHUNK
    sc = lax.dot_general((q * jnp.exp(b)).astype(BF16), (k * jnp.exp(-b)).astype(BF16), _NT,
                         preferred_element_type=F32)
    ti = lax.broadcasted_iota(jnp.int32, (C, C), 0)
    si = lax.broadcasted_iota(jnp.int32, (C, C), 1)
    sc = jnp.where(ti >= si, sc, 0.0)
    return jnp.dot(sc.astype(BF16), v.astype(BF16), preferred_element_type=F32)


def _hgrn_intra_safe(q, k, v, b):
    C, c = HGRN_CHUNK, HGRN_SUB
    half = c // 2
    rows = lax.broadcasted_iota(jnp.int32, (half, HGRN_DIM), 0)
    pieces = [None] * (C // c)
    for j in range(C // c):
        lo, hi = j * c, (j + 1) * c
        bd, qd, kd, vd = b[lo:hi], q[lo:hi], k[lo:hi], v[lo:hi]
        e_j = b[hi - 1:hi, :]
        if hi < C:
            khat = (kd * jnp.exp(e_j - bd)).astype(BF16)
            qt = (q[hi:] * jnp.exp(b[hi:] - e_j)).astype(BF16)
            sc = lax.dot_general(qt, khat, _NT, preferred_element_type=F32)
            upd = jnp.dot(sc.astype(BF16), vd.astype(BF16), preferred_element_type=F32)
            for jj in range(j + 1, C // c):
                u = upd[(jj - j - 1) * c:(jj - j) * c]
                pieces[jj] = u if pieces[jj] is None else pieces[jj] + u
        acc = [jnp.zeros((half, HGRN_DIM), F32), jnp.zeros((half, HGRN_DIM), F32)]
        for s in range(c):
            for hf in range(s // half, 2):
                r0 = hf * half
                diff = bd[r0:r0 + half] - bd[s:s + 1]
                if s >= r0:
                    diff = jnp.where(rows >= s - r0, diff, NEG_BIG)
                col = jnp.sum(qd[r0:r0 + half] * kd[s:s + 1] * jnp.exp(diff), axis=-1, keepdims=True)
                acc[hf] = acc[hf] + col * vd[s:s + 1]
        d = jnp.concatenate(acc, axis=0)
        pieces[j] = d if pieces[j] is None else pieces[j] + d
    return jnp.concatenate(pieces, axis=0)


def _hgrn_finish(o_intra, q, k, v, b, g_in, gn, state_ref):
    C = HGRN_CHUNK
    b_last = b[C - 1:C, :]
    state = state_ref[...]
    o = o_intra + lax.dot_general((q * jnp.exp(b)).astype(BF16), state.astype(BF16), _NT,
                                  preferred_element_type=F32)
    kdec = (k * jnp.exp(b_last - b)).astype(BF16)
    state_ref[...] = state * jnp.exp(b_last) + lax.dot_general(v.astype(BF16), kdec, _TN, preferred_element_type=F32)
    ms = jnp.mean(o * o, axis=-1, keepdims=True)
    return o * lax.rsqrt(ms + RMS_EPS) * gn * (g_in * jax.nn.sigmoid(g_in))


def _hgrn_kernel(q_ref, f_ref, i_ref, g_ref, lb_ref, gn_ref, o_ref, state_ref, *, n_chunks):
    @pl.when(pl.program_id(2) == 0)
    def _():
        state_ref[...] = jnp.zeros_like(state_ref)

    gn = gn_ref[...]
    heads = range(HGRN_HEADS_PER_STEP)
    cols = [slice(hd * HGRN_DIM, (hd + 1) * HGRN_DIM) for hd in heads]

    def body(ci, carry):
        r0 = pl.multiple_of(ci * HGRN_CHUNK, HGRN_CHUNK)
        rs = pl.ds(r0, HGRN_CHUNK)
        gates = [_hgrn_gates(q_ref[rs, cs], f_ref[rs, cs], lb_ref[:, cs]) for cs in cols]
        total = gates[0][2][HGRN_CHUNK - 1:HGRN_CHUNK, :]
        for _, _, b in gates[1:]:
            total = jnp.minimum(total, b[HGRN_CHUNK - 1:HGRN_CHUNK, :])
        mild = jnp.min(total) >= -HGRN_PIVOT_MAX_DECAY

        def run(intra):
            for hd, cs in zip(heads, cols):
                q, k, b = gates[hd]
                v = i_ref[rs, cs]
                y = _hgrn_finish(intra(q, k, v, b), q, k, v, b, g_ref[rs, cs], gn, state_ref.at[hd])
                o_ref[rs, cs] = y.astype(o_ref.dtype)

        pl.when(mild)(lambda: run(_hgrn_intra_pivot))
        pl.when(jnp.logical_not(mild))(lambda: run(_hgrn_intra_safe))
        return carry

    lax.fori_loop(0, n_chunks, body, 0)


def hgrn2(proj, lb, gnorm, layer, batch, seq, tile=512):
    tile = min(tile, seq)
    nt = seq // tile
    w = HGRN_HEADS_PER_STEP * HGRN_DIM
    c0 = (ATTN_WIDTH + 2 * KV_WIDTH) // w
    step = HGRN_WIDTH // w
    spec = lambda k: pl.BlockSpec((tile, w), lambda b, h, t: (b * nt + t, c0 + k * step + h))
    return pl.pallas_call(
        functools.partial(_hgrn_kernel, n_chunks=tile // HGRN_CHUNK),
        grid=(batch, step, nt),
        in_specs=[spec(0), spec(1), spec(2), spec(3),
                  pl.BlockSpec((None, 1, w), lambda b, h, t: (layer, 0, h)),
                  pl.BlockSpec((None, 1, HGRN_DIM), lambda b, h, t: (layer, 0, 0))],
        out_specs=pl.BlockSpec((tile, w), lambda b, h, t: (b * nt + t, h)),
        out_shape=jax.ShapeDtypeStruct((batch * seq, HGRN_WIDTH), BF16),
        scratch_shapes=[pltpu.VMEM((HGRN_HEADS_PER_STEP, HGRN_DIM, HGRN_DIM), F32)],
        compiler_params=_params("parallel", "parallel", "arbitrary"),
        name="hgrn2",
    )(proj, proj, proj, proj, lb, gnorm)


def _lru_kernel(gate_ref, rnn_ref, cw_ref, cb_ref, wa_ref, wx_ref, ba_ref, bx_ref, lam_ref, o_ref,
                xs_ref, h_ref, *, tile):
    @pl.when(pl.program_id(2) == 0)
    def _():
        xs_ref[0:SUBLANES, :] = jnp.zeros((SUBLANES, LRU_HEAD_DIM), F32)
        h_ref[...] = jnp.zeros_like(h_ref)

    x = rnn_ref[...]
    xs_ref[SUBLANES:, :] = x
    u = cb_ref[...] + x * cw_ref[CONV_WIDTH - 1:CONV_WIDTH, :]
    for j in range(CONV_WIDTH - 1):
        back = CONV_WIDTH - 1 - j
        u = u + xs_ref[pl.ds(SUBLANES - back, tile), :] * cw_ref[j:j + 1, :]
    xs_ref[0:SUBLANES, :] = x[tile - SUBLANES:, :]

    ub = u.astype(BF16)
    r = jax.nn.sigmoid(jnp.dot(ub, wa_ref[...], preferred_element_type=F32) + ba_ref[...])
    ig = jax.nn.sigmoid(jnp.dot(ub, wx_ref[...], preferred_element_type=F32) + bx_ref[...])
    nl = -lam_ref[...]
    softplus = jnp.maximum(nl, 0.0) + jnp.log1p(jnp.exp(-jnp.abs(nl)))
    log_a = -RG_C * r * softplus
    a = jnp.exp(log_a)
    bb = jnp.sqrt(jnp.maximum(-jnp.tanh(log_a) * (a * a + 1.0), 0.0)) * (ig * u)

    rows = lax.broadcasted_iota(jnp.int32, (tile, LRU_HEAD_DIM), 0)
    k = 1
    while k < tile:
        keep = rows >= k
        a_sh = jnp.where(keep, pltpu.roll(a, k, 0), 1.0)
        b_sh = jnp.where(keep, pltpu.roll(bb, k, 0), 0.0)
        bb = a * b_sh + bb
        a = a * a_sh
        k *= 2
    h = a * h_ref[...] + bb
    h_ref[...] = h[tile - 1:tile, :]
    o_ref[...] = (h * jax.nn.gelu(gate_ref[...])).astype(o_ref.dtype)


def recurrent_block(proj, conv_w, conv_b, wa, ba, wx, bx, lam, layer, batch, seq, tile=256):
    tile = min(tile, seq)
    nt = seq // tile
    hd = LRU_HEAD_DIM
    vec = pl.BlockSpec((None, 1, hd), lambda b, h, t: (layer, 0, h))
    mat = pl.BlockSpec((None, None, hd, hd), lambda b, h, t: (layer, h, 0, 0))
    return pl.pallas_call(
        functools.partial(_lru_kernel, tile=tile),
        grid=(batch, N_LRU_HEADS, nt),
        in_specs=[pl.BlockSpec((tile, hd), lambda b, h, t: (b * nt + t, h)),
                  pl.BlockSpec((tile, hd), lambda b, h, t: (b * nt + t, N_LRU_HEADS + h)),
                  pl.BlockSpec((None, CONV_WIDTH, hd), lambda b, h, t: (layer, 0, h)),
                  vec, mat, mat, vec, vec, vec],
        out_specs=pl.BlockSpec((tile, hd), lambda b, h, t: (b * nt + t, h)),
        out_shape=jax.ShapeDtypeStruct((batch * seq, LRU_WIDTH), BF16),
        scratch_shapes=[pltpu.VMEM((tile + SUBLANES, hd), F32), pltpu.VMEM((1, hd), F32)],
        compiler_params=_params("parallel", "parallel", "arbitrary"),
        name="recurrent_block",
    )(proj, proj, conv_w, conv_b, wa, wx, ba, bx, lam)


def _expert_kernel(te_ref, tok_ref, nused_ref, ht_hbm, wg_ref, wu_ref, wd_ref, y_ref, xbuf, xb_ref, sem):
    t = pl.program_id(0)
    nused = nused_ref[0]
    T = EXPERT_TILE

    def gather(tile, slot):
        base = tile * T
        for r in range(T):
            tok = tok_ref[base + r]
            src = ht_hbm.at[pl.ds(pl.multiple_of(tok * ROW_CHUNKS, ROW_CHUNKS), ROW_CHUNKS)]
            pltpu.make_async_copy(src, xbuf.at[slot, pl.ds(r * ROW_CHUNKS, ROW_CHUNKS)], sem.at[slot]).start()

    def wait(slot):
        pltpu.make_async_copy(ht_hbm.at[pl.ds(0, T * ROW_CHUNKS)], xbuf.at[slot], sem.at[slot]).wait()

    @pl.when(t == 0)
    def _():
        gather(0, 0)

    slot = lax.rem(t, 2)

    @pl.when(t < nused)
    def _():
        wait(slot)
        gather(jnp.minimum(t + 1, nused - 1), 1 - slot)
        for c in range(ROW_CHUNKS):
            xb_ref[:, c * LANES:(c + 1) * LANES] = xbuf[slot, pl.ds(c, T, stride=ROW_CHUNKS), :].astype(BF16)
        xb = xb_ref[...]
        gate = jnp.dot(xb, wg_ref[...], preferred_element_type=F32)
        up = jnp.dot(xb, wu_ref[...], preferred_element_type=F32)
        hid = (gate * jax.nn.sigmoid(gate) * up).astype(BF16)
        y = jnp.dot(hid, wd_ref[...], preferred_element_type=F32)
        for c in range(ROW_CHUNKS):
            _token_major(y_ref, c, T)[...] = y[:, c * LANES:(c + 1) * LANES]

        @pl.when(t == nused - 1)
        def _():
            wait(1 - slot)

    @pl.when(t >= nused)
    def _():
        y_ref[...] = jnp.zeros_like(y_ref)


def moe_experts(ht, wg, wu, wd, tile_e, slot_tok, n_used, layer):
    n_tiles = tile_e.shape[0]
    T = EXPERT_TILE
    D = wg.shape[2]
    grid_spec = pltpu.PrefetchScalarGridSpec(
        num_scalar_prefetch=3,
        grid=(n_tiles,),
        in_specs=[pl.BlockSpec(memory_space=pl.ANY),
                  pl.BlockSpec((None, None, D, EXPERT_FF), lambda i, te, tok, nu: (layer, te[i], 0, 0)),
                  pl.BlockSpec((None, None, D, EXPERT_FF), lambda i, te, tok, nu: (layer, te[i], 0, 0)),
                  pl.BlockSpec((None, None, EXPERT_FF, D), lambda i, te, tok, nu: (layer, te[i], 0, 0))],
        out_specs=pl.BlockSpec((T * ROW_CHUNKS, LANES), lambda i, te, tok, nu: (i, 0)),
        scratch_shapes=[pltpu.VMEM((2, T * ROW_CHUNKS, LANES), F32), pltpu.VMEM((T, D), BF16),
                        pltpu.SemaphoreType.DMA((2,))],
    )
    return pl.pallas_call(
        _expert_kernel,
        grid_spec=grid_spec,
        out_shape=jax.ShapeDtypeStruct((n_tiles * T * ROW_CHUNKS, LANES), F32),
        compiler_params=_params("arbitrary"),
        name="moe_experts",
    )(tile_e, slot_tok, n_used, ht, wg, wu, wd)


def _combine_ln_kernel(dest_ref, y_hbm, ht_ref, wt_ref, g_ref, b_ref, of_ref, ob_ref, ybuf, sem, *, tm, nsteps):
    t = pl.program_id(0)

    def gather(step, slot):
        base = step * (tm * TOP_K)
        for r in range(tm):
            for k in range(TOP_K):
                d = dest_ref[base + r * TOP_K + k]
                src = y_hbm.at[pl.ds(pl.multiple_of(d * ROW_CHUNKS, ROW_CHUNKS), ROW_CHUNKS)]
                pltpu.make_async_copy(src, ybuf.at[slot, k, pl.ds(r * ROW_CHUNKS, ROW_CHUNKS)],
                                      sem.at[slot]).start()

    def wait(slot):
        for k in range(TOP_K):
            pltpu.make_async_copy(y_hbm.at[pl.ds(0, tm * ROW_CHUNKS)], ybuf.at[slot, k], sem.at[slot]).wait()

    @pl.when(t == 0)
    def _():
        gather(0, 0)

    slot = lax.rem(t, 2)
    wait(slot)
    gather(jnp.minimum(t + 1, nsteps - 1), 1 - slot)
    w0 = wt_ref[:, 0:1]
    w1 = wt_ref[:, 1:2]
    for c in range(ROW_CHUNKS):
        y0 = ybuf[slot, 0, pl.ds(c, tm, stride=ROW_CHUNKS), :]
        y1 = ybuf[slot, 1, pl.ds(c, tm, stride=ROW_CHUNKS), :]
        of_ref[:, c * LANES:(c + 1) * LANES] = ALPHA * _token_major(ht_ref, c, tm)[...] + (y0 * w0 + y1 * w1)
    mu, rstd = _ln_stats(lambda c: of_ref[:, c * LANES:(c + 1) * LANES], ROW_CHUNKS)
    for c in range(ROW_CHUNKS):
        sl = slice(c * LANES, (c + 1) * LANES)
        y = (of_ref[:, sl] - mu) * rstd * g_ref[:, sl] + b_ref[:, sl]
        of_ref[:, sl] = y
        ob_ref[:, sl] = y.astype(BF16)

    @pl.when(t == nsteps - 1)
    def _():
        wait(1 - slot)


def moe_combine_ln(y, ht, wts, dest, g, b, layer, tm=256):
    N = wts.shape[0]
    D = g.shape[2]
    tm = min(tm, N)
    nsteps = N // tm
    vec = pl.BlockSpec((None, 1, D), lambda i, d: (layer, 0, 0))
    grid_spec = pltpu.PrefetchScalarGridSpec(
        num_scalar_prefetch=1,
        grid=(nsteps,),
        in_specs=[pl.BlockSpec(memory_space=pl.ANY),
                  pl.BlockSpec((tm * ROW_CHUNKS, LANES), lambda i, d: (i, 0)),
                  pl.BlockSpec((tm, LANES), lambda i, d: (i, 0)),
                  vec, vec],
        out_specs=[pl.BlockSpec((tm, D), lambda i, d: (i, 0)),
                   pl.BlockSpec((tm, D), lambda i, d: (i, 0))],
        scratch_shapes=[pltpu.VMEM((2, TOP_K, tm * ROW_CHUNKS, LANES), F32), pltpu.SemaphoreType.DMA((2,))],
    )
    return pl.pallas_call(
        functools.partial(_combine_ln_kernel, tm=tm, nsteps=nsteps),
        grid_spec=grid_spec,
        out_shape=[jax.ShapeDtypeStruct((N, D), F32), jax.ShapeDtypeStruct((N, D), BF16)],
        compiler_params=_params("arbitrary"),
        name="moe_combine_ln",
    )(dest, y, ht, wts, g, b)


def _dispatch_plan(e_idx):
    N = e_idx.shape[0]
    A = N * TOP_K
    T = EXPERT_TILE
    n_tiles = (A + N_EXPERTS * (T - 1) + T - 1) // T
    flat_e = e_idx.reshape(-1)
    onehot = jax.nn.one_hot(flat_e, N_EXPERTS, dtype=jnp.int32)
    counts = jnp.sum(onehot, axis=0)
    rank = jnp.sum((jnp.cumsum(onehot, axis=0) - 1) * onehot, axis=1)
    padded = (counts + T - 1) // T * T
    pad_end = jnp.cumsum(padded)
    dest = (pad_end[flat_e] - padded[flat_e] + rank).astype(jnp.int32)
    slot_tok = jnp.zeros((n_tiles * T,), jnp.int32).at[dest].set(jnp.arange(A, dtype=jnp.int32) // TOP_K)
    starts = jnp.arange(n_tiles, dtype=jnp.int32) * T
    tile_e = jnp.minimum(jnp.sum((pad_end[None, :] <= starts[:, None]).astype(jnp.int32), axis=1), N_EXPERTS - 1)
    n_used = (pad_end[-1] // T).astype(jnp.int32).reshape(1)
    return tile_e.astype(jnp.int32), slot_tok, n_used, dest


def _rotary_table(positions):
    inv_freq = ROPE_THETA ** (-jnp.arange(0, ROT_DIM, 2, dtype=F32) / ROT_DIM)
    ang = positions.astype(F32).reshape(-1)[:, None] * inv_freq
    cos, sin = jnp.cos(ang), jnp.sin(ang)
    n = ang.shape[0]
    rest = HEAD_DIM - ROT_DIM
    z = lambda w: jnp.zeros((n, w), F32)
    c = jnp.concatenate([cos, cos, jnp.ones((n, rest), F32)], axis=1)
    s_lo = jnp.concatenate([-sin, z(HEAD_DIM - ROT_HALF)], axis=1)
    s_hi = jnp.concatenate([z(ROT_HALF), sin, z(rest)], axis=1)
    rep = LANES // HEAD_DIM
    return jnp.concatenate([jnp.tile(c, (1, rep)), jnp.tile(s_lo, (1, rep)), jnp.tile(s_hi, (1, rep))], axis=1)


def kernel(x, positions, even_w_in, even_w_out, attn_sinks, hgrn_lb_logits, hgrn_gnorm_w, rec_w_in, rec_conv_w, rec_conv_b, rec_gate_a_w, rec_gate_a_b, rec_gate_x_w, rec_gate_x_b, rec_lambda, rec_w_out, ln_mix_g, ln_mix_b, ln_ffn_g, ln_ffn_b, router_group_w, router_group_b, router_expert_w, router_expert_b, moe_w_gate, moe_w_up, moe_w_down):
    B, S, D = x.shape
    N = B * S
    tab = _rotary_table(positions)
    sm = jax.nn.softmax(hgrn_lb_logits.astype(F32), axis=0)
    lb_table = (jnp.cumsum(sm, axis=0) - sm[:1])[:, None, :]
    pad = LANES - N_GROUPS - N_EXPERTS
    router_w = jnp.concatenate([router_group_w, router_expert_w, jnp.zeros((DEPTH, D, pad), F32)], axis=-1)
    router_b = jnp.concatenate([router_group_b, router_expert_b, jnp.zeros((DEPTH, pad), F32)], axis=-1)[:, None, :]
    row = lambda a: a[:, None, :]
    even_w_in_b, even_w_out_b = even_w_in.astype(BF16), even_w_out.astype(BF16)
    rec_w_in_b, rec_w_out_b = rec_w_in.astype(BF16), rec_w_out.astype(BF16)
    wa_b, wx_b = rec_gate_a_w.astype(BF16), rec_gate_x_w.astype(BF16)
    wg_b, wu_b, wd_b = moe_w_gate.astype(BF16), moe_w_up.astype(BF16), moe_w_down.astype(BF16)
    gnorm = row(hgrn_gnorm_w)
    ln_mix_g3, ln_mix_b3, ln_ffn_g3, ln_ffn_b3 = row(ln_mix_g), row(ln_mix_b), row(ln_ffn_g), row(ln_ffn_b)

    h = x.reshape(N, D)
    hb = h.astype(BF16)
    for layer in range(DEPTH):
        j = layer // 2
        if layer % 2 == 0:
            proj = matmul(hb, even_w_in_b, j, F32)
            o_a = swa_attention(proj, tab, attn_sinks[j], B, S)
            o_b = hgrn2(proj, lb_table, gnorm, j, B, S)
            mixed = jnp.concatenate([o_a, o_b], axis=1)
            w_out = even_w_out_b
        else:
            proj = matmul(hb, rec_w_in_b, j, F32)
            mixed = recurrent_block(proj, rec_conv_w, row(rec_conv_b), wa_b, row(rec_gate_a_b), wx_b,
                                    row(rec_gate_x_b), row(rec_lambda), j, B, S)
            w_out = rec_w_out_b
        ht, idx, wts = matmul_residual_ln_route(mixed, w_out, h, ln_mix_g3, ln_mix_b3, router_w, router_b, layer)
        tile_e, slot_tok, n_used, dest = _dispatch_plan(idx[:, :TOP_K])
        y = moe_experts(ht, wg_b, wu_b, wd_b, tile_e, slot_tok, n_used, layer)
        h, hb = moe_combine_ln(y, ht, wts, dest, ln_ffn_g3, ln_ffn_b3, layer)
    return h.reshape(B, S, D)
```

```python
import functools

import jax
import jax.numpy as jnp
from jax import lax
from jax.experimental import pallas as pl
from jax.experimental.pallas import tpu as pltpu

F32 = jnp.float32
BF16 = jnp.bfloat16

D_MODEL = 4096
DEPTH = 4
ATTN_WIDTH = D_MODEL // 2
HEAD_DIM = 64
N_Q_HEADS = ATTN_WIDTH // HEAD_DIM
N_KV_HEADS = N_Q_HEADS // 8
Q_PER_KV = N_Q_HEADS // N_KV_HEADS
KV_WIDTH = N_KV_HEADS * HEAD_DIM
WINDOW = 128
ROT_DIM = HEAD_DIM // 4
ROT_HALF = ROT_DIM // 2
ROPE_THETA = 500000.0
MASK_VALUE = -1e9
HGRN_WIDTH = D_MODEL - ATTN_WIDTH
HGRN_DIM = 128
N_HGRN_HEADS = HGRN_WIDTH // HGRN_DIM
HGRN_CHUNK = 64
HGRN_SUB = 16
HGRN_HEADS_PER_BLOCK = 4
HGRN_BLOCKS_PER_STEP = 2
HGRN_PIVOT_MAX_DECAY = 75.0
LRU_WIDTH = D_MODEL
N_LRU_HEADS = 16
LRU_HEAD_DIM = LRU_WIDTH // N_LRU_HEADS
CONV_WIDTH = 4
RG_C = 8.0
N_GROUPS = 4
EXPERTS_PER_GROUP = 8
N_EXPERTS = N_GROUPS * EXPERTS_PER_GROUP
TOP_K = 2
EXPERT_FF = 3 * D_MODEL // 32
EXPERT_TILE = 256
ALPHA = (2.0 * DEPTH) ** 0.25
LN_EPS = 1e-5
RMS_EPS = 1e-6

LANES = 128
SUBLANES = 8
VMEM_LIMIT = 56 * 1024 * 1024
NEG_BIG = -1e30

_NT = (((1,), (1,)), ((), ()))
_TN = (((0,), (0,)), ((), ()))


def _params(*sem):
    return pltpu.CompilerParams(dimension_semantics=sem, vmem_limit_bytes=VMEM_LIMIT)


def _mm_kernel(x_ref, w_ref, o_ref):
    o_ref[...] = jnp.dot(x_ref[...], w_ref[...], preferred_element_type=F32).astype(o_ref.dtype)


def matmul(x, w, layer, out_dtype, tm=1024, tn=512):
    M, K = x.shape
    N = w.shape[2]
    tm = min(tm, M)
    return pl.pallas_call(
        _mm_kernel,
        grid=(M // tm, N // tn),
        in_specs=[pl.BlockSpec((tm, K), lambda i, j: (i, 0)),
                  pl.BlockSpec((None, K, tn), lambda i, j: (layer, 0, j))],
        out_specs=pl.BlockSpec((tm, tn), lambda i, j: (i, j)),
        out_shape=jax.ShapeDtypeStruct((M, N), out_dtype),
        compiler_params=_params("parallel", "arbitrary"),
        name="matmul",
    )(x, w)


LN_CHUNK = 512


def _layer_norm_inplace(of_ref, g_ref, b_ref, ob_ref=None):
    D = of_ref.shape[1]
    cols = [slice(c, c + LN_CHUNK) for c in range(0, D, LN_CHUNK)]
    acc = None
    for sl in cols:
        acc = of_ref[:, sl] if acc is None else acc + of_ref[:, sl]
    mu = jnp.sum(acc, axis=-1, keepdims=True) * (1.0 / D)
    acc = None
    for sl in cols:
        d = of_ref[:, sl] - mu
        acc = d * d if acc is None else acc + d * d
    rstd = lax.rsqrt(jnp.sum(acc, axis=-1, keepdims=True) * (1.0 / D) + LN_EPS)
    for sl in cols:
        y = (of_ref[:, sl] - mu) * rstd * g_ref[:, sl] + b_ref[:, sl]
        of_ref[:, sl] = y
        if ob_ref is not None:
            ob_ref[:, sl] = y.astype(BF16)


ROUTE_ROWS = 256


def _route(h, w_ref, b_ref):
    n = h.shape[0]
    h_hi = h.astype(BF16)
    h_lo = (h - h_hi.astype(F32)).astype(BF16)
    parts = jnp.dot(jnp.concatenate([h_hi, h_lo], axis=0), w_ref[...], preferred_element_type=F32)
    logits = (parts[:n, :LANES] + (parts[:n, LANES:] + parts[n:, :LANES]) + parts[n:, LANES:]) + b_ref[...]
    lane = lax.broadcasted_iota(jnp.int32, logits.shape, 1)
    neg = -jnp.inf
    big = jnp.int32(LANES)
    gl = jnp.where(lane < N_GROUPS, logits, neg)
    gmax = jnp.max(gl, axis=-1, keepdims=True)
    g_sel = jnp.min(jnp.where(gl == gmax, lane, big), axis=-1, keepdims=True)
    p_top = 1.0 / jnp.sum(jnp.exp(gl - gmax), axis=-1, keepdims=True)
    lo = N_GROUPS + g_sel * EXPERTS_PER_GROUP
    el = jnp.where((lane >= lo) & (lane < lo + EXPERTS_PER_GROUP), logits, neg)
    m1 = jnp.max(el, axis=-1, keepdims=True)
    i1 = jnp.min(jnp.where(el == m1, lane, big), axis=-1, keepdims=True)
    el2 = jnp.where(lane == i1, neg, el)
    m2 = jnp.max(el2, axis=-1, keepdims=True)
    i2 = jnp.min(jnp.where(el2 == m2, lane, big), axis=-1, keepdims=True)
    z = jnp.sum(jnp.exp(el - m1), axis=-1, keepdims=True)
    p1 = 1.0 / z
    p2 = jnp.exp(m2 - m1) / z
    tot = p1 + p2
    idx = jnp.where(lane == 0, i1 - N_GROUPS, jnp.where(lane == 1, i2 - N_GROUPS, 0))
    wt = jnp.where(lane == 0, p1 / tot * p_top, jnp.where(lane == 1, p2 / tot * p_top, 0.0))
    return idx, wt


def _mm_ln_route_kernel(x_ref, w_ref, res_ref, g_ref, b_ref, rw_ref, rb_ref, of_ref, idx_ref, wt_ref, *, nj, tn):
    j = pl.program_id(1)
    pre = ALPHA * res_ref[...] + jnp.dot(x_ref[...], w_ref[...], preferred_element_type=F32)
    for c in range(nj):
        @pl.when(j == c)
        def _(c=c):
            of_ref[:, c * tn:(c + 1) * tn] = pre

    @pl.when(j == nj - 1)
    def _():
        _layer_norm_inplace(of_ref, g_ref, b_ref)

        def route_rows(r, carry):
            rows = pl.ds(pl.multiple_of(r * ROUTE_ROWS, ROUTE_ROWS), ROUTE_ROWS)
            idx_ref[rows, :], wt_ref[rows, :] = _route(of_ref[rows, :], rw_ref, rb_ref)
            return carry

        lax.fori_loop(0, of_ref.shape[0] // ROUTE_ROWS, route_rows, 0)


def matmul_residual_ln_route(x, w, res, g, b, rw, rb, layer, tm=512, tn=512):
    M, K = x.shape
    N = w.shape[2]
    tm = min(tm, M)
    nj = N // tn
    vec = pl.BlockSpec((None, 1, N), lambda i, j: (layer, 0, 0))
    return pl.pallas_call(
        functools.partial(_mm_ln_route_kernel, nj=nj, tn=tn),
        grid=(M // tm, nj),
        in_specs=[pl.BlockSpec((tm, K), lambda i, j: (i, 0)),
                  pl.BlockSpec((None, K, tn), lambda i, j: (layer // 2, 0, j)),
                  pl.BlockSpec((tm, tn), lambda i, j: (i, j)),
                  vec, vec,
                  pl.BlockSpec((None, N, 2 * LANES), lambda i, j: (layer, 0, 0)),
                  pl.BlockSpec((None, 1, LANES), lambda i, j: (layer, 0, 0))],
        out_specs=[pl.BlockSpec((tm, N), lambda i, j: (i, 0)),
                   pl.BlockSpec((tm, LANES), lambda i, j: (i, 0)),
                   pl.BlockSpec((tm, LANES), lambda i, j: (i, 0))],
        out_shape=[jax.ShapeDtypeStruct((M, N), F32),
                   jax.ShapeDtypeStruct((M, LANES), jnp.int32),
                   jax.ShapeDtypeStruct((M, LANES), F32)],
        compiler_params=_params("parallel", "arbitrary"),
        name="matmul_residual_ln_route",
    )(x, w, res, g, b, rw, rb)


def _rotary(x, tab):
    c, s_lo, s_hi = tab[:, :LANES], tab[:, LANES:2 * LANES], tab[:, 2 * LANES:]
    out = []
    for i in range(x.shape[1] // LANES):
        xc = x[:, i * LANES:(i + 1) * LANES]
        out.append(xc * c + pltpu.roll(xc, LANES - ROT_HALF, 1) * s_lo + pltpu.roll(xc, ROT_HALF, 1) * s_hi)
    return out


def _attn_kernel(sink_ref, q_ref, kc_ref, vc_ref, kp_ref, vp_ref, tc_ref, tp_ref, mask_ref, o_ref):
    W = WINDOW
    tab_c = tc_ref[...]
    tab_p = tp_ref[...]
    q_cols = _rotary(q_ref[...], tab_c)
    kc = jnp.concatenate(_rotary(kc_ref[...], tab_c), axis=1)
    kp = jnp.concatenate(_rotary(kp_ref[...], tab_p), axis=1)
    k_all = jnp.concatenate([kp, kc], axis=0).astype(BF16)
    v_all = jnp.concatenate([vp_ref[...], vc_ref[...]], axis=0).astype(BF16)
    outs = []
    for kv in range(N_KV_HEADS):
        heads = range(kv * Q_PER_KV, (kv + 1) * Q_PER_KV)
        qg = jnp.concatenate(
            [q_cols[h // 2][:, (h % 2) * HEAD_DIM:(h % 2 + 1) * HEAD_DIM] for h in heads], axis=0).astype(BF16)
        kh = k_all[:, kv * HEAD_DIM:(kv + 1) * HEAD_DIM]
        vh = v_all[:, kv * HEAD_DIM:(kv + 1) * HEAD_DIM]
        s_all = lax.dot_general(qg, kh, _NT, preferred_element_type=F32)
        ps, denoms = [], []
        for g, h in enumerate(heads):
            s = s_all[g * W:(g + 1) * W] * mask_ref[:, :2 * W] + mask_ref[:, 2 * W:4 * W]
            sink = sink_ref[h]
            m = jnp.maximum(jnp.max(s, axis=-1, keepdims=True), sink)
            p = jnp.exp(s - m) * mask_ref[:, 4 * W:]
            denoms.append(jnp.sum(p, axis=-1, keepdims=True) + jnp.exp(sink - m))
            ps.append(p.astype(BF16))
        o_all = jnp.dot(jnp.concatenate(ps, axis=0), vh, preferred_element_type=F32)
        for g in range(Q_PER_KV):
            outs.append(o_all[g * W:(g + 1) * W] / denoms[g])
    o_ref[...] = jnp.concatenate(outs, axis=1).astype(o_ref.dtype)


def _band_masks():
    qi = jnp.arange(WINDOW)[:, None]
    kj = jnp.arange(2 * WINDOW)[None, :]
    rel = qi + WINDOW - kj
    in_band = (rel >= 0) & (rel < WINDOW)
    out = []
    for has_prev in (False, True):
        keep = (in_band & ((kj >= WINDOW) | has_prev)).astype(F32)
        out.append(jnp.concatenate([keep * (HEAD_DIM ** -0.5), (1.0 - keep) * MASK_VALUE, keep], axis=1))
    return jnp.stack(out)


def swa_attention(proj, tab, sinks, batch, seq):
    nb = seq // WINDOW
    kcol = ATTN_WIDTH // KV_WIDTH
    cur = lambda b, n: b * nb + n
    prev = lambda b, n: b * nb + jnp.maximum(n - 1, 0)
    return pl.pallas_call(
        _attn_kernel,
        grid=(batch, nb),
        in_specs=[pl.BlockSpec(memory_space=pltpu.SMEM),
                  pl.BlockSpec((WINDOW, ATTN_WIDTH), lambda b, n: (cur(b, n), 0)),
                  pl.BlockSpec((WINDOW, KV_WIDTH), lambda b, n: (cur(b, n), kcol)),
                  pl.BlockSpec((WINDOW, KV_WIDTH), lambda b, n: (cur(b, n), kcol + 1)),
                  pl.BlockSpec((WINDOW, KV_WIDTH), lambda b, n: (prev(b, n), kcol)),
                  pl.BlockSpec((WINDOW, KV_WIDTH), lambda b, n: (prev(b, n), kcol + 1)),
                  pl.BlockSpec((WINDOW, 3 * LANES), lambda b, n: (cur(b, n), 0)),
                  pl.BlockSpec((WINDOW, 3 * LANES), lambda b, n: (prev(b, n), 0)),
                  pl.BlockSpec((None, WINDOW, 6 * WINDOW), lambda b, n: (jnp.minimum(n, 1), 0, 0))],
        out_specs=pl.BlockSpec((WINDOW, ATTN_WIDTH), lambda b, n: (cur(b, n), 0)),
        out_shape=jax.ShapeDtypeStruct((batch * seq, ATTN_WIDTH), BF16),
        compiler_params=_params("parallel", "parallel"),
        name="swa_attention",
    )(sinks, proj, proj, proj, proj, proj, tab, tab, _band_masks())


def _hgrn_gates(q_in, z, lb):
    C = HGRN_CHUNK
    sg = jax.nn.sigmoid(z)
    lf = jnp.log(lb + (1.0 - lb) * sg)
    k = (1.0 - lb) * jax.nn.sigmoid(-z)
    q = q_in * jax.nn.sigmoid(q_in)
    ti = lax.broadcasted_iota(jnp.int32, (C, C), 0)
    si = lax.broadcasted_iota(jnp.int32, (C, C), 1)
    tril = jnp.where(ti >= si, 1.0, 0.0).astype(F32)
    b = jnp.dot(tril, lf, preferred_element_type=F32, precision=lax.Precision.HIGHEST)
    return q, k, b


def _hgrn_intra_pivot(q, k, v, b):
    C = HGRN_CHUNK
    sc = lax.dot_general((q * jnp.exp(b)).astype(BF16), (k * jnp.exp(-b)).astype(BF16), _NT,
                         preferred_element_type=F32)
    ti = lax.broadcasted_iota(jnp.int32, (C, C), 0)
    si = lax.broadcasted_iota(jnp.int32, (C, C), 1)
    sc = jnp.where(ti >= si, sc, 0.0)
    return jnp.dot(sc.astype(BF16), v.astype(BF16), preferred_element_type=F32)


def _hgrn_intra_safe(q, k, v, b):
    C, c = HGRN_CHUNK, HGRN_SUB
    half = c // 2
    rows = lax.broadcasted_iota(jnp.int32, (half, HGRN_DIM), 0)
    pieces = [None] * (C // c)
    for j in range(C // c):
        lo, hi = j * c, (j + 1) * c
        bd, qd, kd, vd = b[lo:hi], q[lo:hi], k[lo:hi], v[lo:hi]
        e_j = b[hi - 1:hi, :]
        if hi < C:
            khat = (kd * jnp.exp(e_j - bd)).astype(BF16)
            qt = (q[hi:] * jnp.exp(b[hi:] - e_j)).astype(BF16)
            sc = lax.dot_general(qt, khat, _NT, preferred_element_type=F32)
            upd = jnp.dot(sc.astype(BF16), vd.astype(BF16), preferred_element_type=F32)
            for jj in range(j + 1, C // c):
                u = upd[(jj - j - 1) * c:(jj - j) * c]
                pieces[jj] = u if pieces[jj] is None else pieces[jj] + u
        acc = [jnp.zeros((half, HGRN_DIM), F32), jnp.zeros((half, HGRN_DIM), F32)]
        for s in range(c):
            for hf in range(s // half, 2):
                r0 = hf * half
                diff = bd[r0:r0 + half] - bd[s:s + 1]
                if s >= r0:
                    diff = jnp.where(rows >= s - r0, diff, NEG_BIG)
                col = jnp.sum(qd[r0:r0 + half] * kd[s:s + 1] * jnp.exp(diff), axis=-1, keepdims=True)
                acc[hf] = acc[hf] + col * vd[s:s + 1]
        d = jnp.concatenate(acc, axis=0)
        pieces[j] = d if pieces[j] is None else pieces[j] + d
    return jnp.concatenate(pieces, axis=0)


def _hgrn_finish(o_intra, q, k, v, b, g_in, gn, state_ref):
    C = HGRN_CHUNK
    b_last = b[C - 1:C, :]
    state = state_ref[...]
    o = o_intra + lax.dot_general((q * jnp.exp(b)).astype(BF16), state.astype(BF16), _NT,
                                  preferred_element_type=F32)
    kdec = (k * jnp.exp(b_last - b)).astype(BF16)
    state_ref[...] = state * jnp.exp(b_last) + lax.dot_general(v.astype(BF16), kdec, _TN, preferred_element_type=F32)
    ms = jnp.mean(o * o, axis=-1, keepdims=True)
    return o * lax.rsqrt(ms + RMS_EPS) * gn * (g_in * jax.nn.sigmoid(g_in))


def _hgrn_kernel(*refs, n_chunks):
    nb, hb = HGRN_BLOCKS_PER_STEP, HGRN_HEADS_PER_BLOCK
    q_refs, f_refs, i_refs, g_refs, lb_refs = (refs[k * nb:(k + 1) * nb] for k in range(5))
    gn_ref = refs[5 * nb]
    o_refs = refs[5 * nb + 1:6 * nb + 1]
    state_ref = refs[6 * nb + 1]

    @pl.when(pl.program_id(2) == 0)
    def _():
        state_ref[...] = jnp.zeros_like(state_ref)

    gn = gn_ref[...]
    heads = [(blk, slice(hd * HGRN_DIM, (hd + 1) * HGRN_DIM)) for blk in range(nb) for hd in range(hb)]

    def body(ci, carry):
        r0 = pl.multiple_of(ci * HGRN_CHUNK, HGRN_CHUNK)
        rs = pl.ds(r0, HGRN_CHUNK)
        gates = [_hgrn_gates(q_refs[blk][rs, cs], f_refs[blk][rs, cs], lb_refs[blk][:, cs]) for blk, cs in heads]
        total = gates[0][2][HGRN_CHUNK - 1:HGRN_CHUNK, :]
        for _, _, b in gates[1:]:
            total = jnp.minimum(total, b[HGRN_CHUNK - 1:HGRN_CHUNK, :])
        mild = jnp.min(total) >= -HGRN_PIVOT_MAX_DECAY

        def run(intra):
            for n, (blk, cs) in enumerate(heads):
                q, k, b = gates[n]
                v = i_refs[blk][rs, cs]
                y = _hgrn_finish(intra(q, k, v, b), q, k, v, b, g_refs[blk][rs, cs], gn, state_ref.at[n])
                o_refs[blk][rs, cs] = y.astype(o_refs[blk].dtype)

        pl.when(mild)(lambda: run(_hgrn_intra_pivot))
        pl.when(jnp.logical_not(mild))(lambda: run(_hgrn_intra_safe))
        return carry

    lax.fori_loop(0, n_chunks, body, 0)


def hgrn2(proj, lb, gnorm, layer, batch, seq, tile=512):
    tile = min(tile, seq)
    nt = seq // tile
    nb = HGRN_BLOCKS_PER_STEP
    w = HGRN_HEADS_PER_BLOCK * HGRN_DIM
    c0 = (ATTN_WIDTH + 2 * KV_WIDTH) // w
    per_kind = HGRN_WIDTH // w
    per_slab = per_kind // nb
    spec = lambda k, blk: pl.BlockSpec(
        (tile, w), lambda b, h, t: (b * nt + t, c0 + k * per_kind + blk * per_slab + h))
    blocks = range(nb)
    return pl.pallas_call(
        functools.partial(_hgrn_kernel, n_chunks=tile // HGRN_CHUNK),
        grid=(batch, per_slab, nt),
        in_specs=([spec(k, blk) for k in range(4) for blk in blocks]
                  + [pl.BlockSpec((None, 1, w), lambda b, h, t, blk=blk: (layer, 0, blk * per_slab + h))
                     for blk in blocks]
                  + [pl.BlockSpec((None, 1, HGRN_DIM), lambda b, h, t: (layer, 0, 0))]),
        out_specs=[pl.BlockSpec((tile, w), lambda b, h, t: (b * nt + t, h)) for _ in blocks],
        out_shape=[jax.ShapeDtypeStruct((batch * seq, HGRN_WIDTH // nb), BF16) for _ in blocks],
        scratch_shapes=[pltpu.VMEM((nb * HGRN_HEADS_PER_BLOCK, HGRN_DIM, HGRN_DIM), F32)],
        compiler_params=_params("parallel", "parallel", "arbitrary"),
        name="hgrn2",
    )(*([proj] * (4 * nb)), *([lb] * nb), gnorm)


def _lru_kernel(gate_ref, rnn_ref, cw_ref, cb_ref, wa_ref, wx_ref, ba_ref, bx_ref, lam_ref, o_ref,
                xs_ref, h_ref, *, tile):
    @pl.when(pl.program_id(2) == 0)
    def _():
        xs_ref[0:SUBLANES, :] = jnp.zeros((SUBLANES, LRU_HEAD_DIM), F32)
        h_ref[...] = jnp.zeros_like(h_ref)

    x = rnn_ref[...]
    xs_ref[SUBLANES:, :] = x
    u = cb_ref[...] + x * cw_ref[CONV_WIDTH - 1:CONV_WIDTH, :]
    for j in range(CONV_WIDTH - 1):
        back = CONV_WIDTH - 1 - j
        u = u + xs_ref[pl.ds(SUBLANES - back, tile), :] * cw_ref[j:j + 1, :]
    xs_ref[0:SUBLANES, :] = x[tile - SUBLANES:, :]

    ub = u.astype(BF16)
    r = jax.nn.sigmoid(jnp.dot(ub, wa_ref[...], preferred_element_type=F32) + ba_ref[...])
    ig = jax.nn.sigmoid(jnp.dot(ub, wx_ref[...], preferred_element_type=F32) + bx_ref[...])
    nl = -lam_ref[...]
    softplus = jnp.maximum(nl, 0.0) + jnp.log1p(jnp.exp(-jnp.abs(nl)))
    log_a = -RG_C * r * softplus
    a = jnp.exp(log_a)
    bb = jnp.sqrt(jnp.maximum(-jnp.tanh(log_a) * (a * a + 1.0), 0.0)) * (ig * u)

    rows = lax.broadcasted_iota(jnp.int32, (tile, LRU_HEAD_DIM), 0)
    k = 1
    while k < tile:
        keep = rows >= k
        a_sh = jnp.where(keep, pltpu.roll(a, k, 0), 1.0)
        b_sh = jnp.where(keep, pltpu.roll(bb, k, 0), 0.0)
        bb = a * b_sh + bb
        a = a * a_sh
        k *= 2
    h = a * h_ref[...] + bb
    h_ref[...] = h[tile - 1:tile, :]
    o_ref[...] = (h * jax.nn.gelu(gate_ref[...])).astype(o_ref.dtype)


def recurrent_block(proj, conv_w, conv_b, wa, ba, wx, bx, lam, layer, batch, seq, tile=256):
    tile = min(tile, seq)
    nt = seq // tile
    hd = LRU_HEAD_DIM
    vec = pl.BlockSpec((None, 1, hd), lambda b, h, t: (layer, 0, h))
    mat = pl.BlockSpec((None, None, hd, hd), lambda b, h, t: (layer, h, 0, 0))
    return pl.pallas_call(
        functools.partial(_lru_kernel, tile=tile),
        grid=(batch, N_LRU_HEADS, nt),
        in_specs=[pl.BlockSpec((tile, hd), lambda b, h, t: (b * nt + t, h)),
                  pl.BlockSpec((tile, hd), lambda b, h, t: (b * nt + t, N_LRU_HEADS + h)),
                  pl.BlockSpec((None, CONV_WIDTH, hd), lambda b, h, t: (layer, 0, h)),
                  vec, mat, mat, vec, vec, vec],
        out_specs=pl.BlockSpec((tile, hd), lambda b, h, t: (b * nt + t, h)),
        out_shape=jax.ShapeDtypeStruct((batch * seq, LRU_WIDTH), BF16),
        scratch_shapes=[pltpu.VMEM((tile + SUBLANES, hd), F32), pltpu.VMEM((1, hd), F32)],
        compiler_params=_params("parallel", "parallel", "arbitrary"),
        name="recurrent_block",
    )(proj, proj, conv_w, conv_b, wa, wx, ba, bx, lam)


def _expert_kernel(te_ref, tok_ref, nused_ref, h_hbm, wg_ref, wu_ref, wd_ref, y_ref, xbuf, sem):
    t = pl.program_id(0)
    nused = nused_ref[0]
    T = EXPERT_TILE

    def gather(tile, slot):
        base = tile * T
        for r in range(T):
            tok = tok_ref[base + r]
            pltpu.make_async_copy(h_hbm.at[pl.ds(tok, 1)], xbuf.at[slot, pl.ds(r, 1)], sem.at[slot]).start()

    def wait(slot):
        pltpu.make_async_copy(h_hbm.at[pl.ds(0, T)], xbuf.at[slot], sem.at[slot]).wait()

    @pl.when(t == 0)
    def _():
        gather(0, 0)

    slot = lax.rem(t, 2)

    @pl.when(t < nused)
    def _():
        wait(slot)
        gather(jnp.minimum(t + 1, nused - 1), 1 - slot)
        xb = xbuf[slot].astype(BF16)
        gate = jnp.dot(xb, wg_ref[...], preferred_element_type=F32)
        up = jnp.dot(xb, wu_ref[...], preferred_element_type=F32)
        hid = (gate * jax.nn.sigmoid(gate) * up).astype(BF16)
        y_ref[...] = jnp.dot(hid, wd_ref[...], preferred_element_type=F32)

        @pl.when(t == nused - 1)
        def _():
            wait(1 - slot)

    @pl.when(t >= nused)
    def _():
        y_ref[...] = jnp.zeros_like(y_ref)


def moe_experts(h, wg, wu, wd, tile_e, slot_tok, n_used, layer):
    n_tiles = tile_e.shape[0]
    T = EXPERT_TILE
    D = wg.shape[2]
    grid_spec = pltpu.PrefetchScalarGridSpec(
        num_scalar_prefetch=3,
        grid=(n_tiles,),
        in_specs=[pl.BlockSpec(memory_space=pl.ANY),
                  pl.BlockSpec((None, None, D, EXPERT_FF), lambda i, te, tok, nu: (layer, te[i], 0, 0)),
                  pl.BlockSpec((None, None, D, EXPERT_FF), lambda i, te, tok, nu: (layer, te[i], 0, 0)),
                  pl.BlockSpec((None, None, EXPERT_FF, D), lambda i, te, tok, nu: (layer, te[i], 0, 0))],
        out_specs=pl.BlockSpec((T, D), lambda i, te, tok, nu: (i, 0)),
        scratch_shapes=[pltpu.VMEM((2, T, D), F32), pltpu.SemaphoreType.DMA((2,))],
    )
    return pl.pallas_call(
        _expert_kernel,
        grid_spec=grid_spec,
        out_shape=jax.ShapeDtypeStruct((n_tiles * T, D), F32),
        compiler_params=_params("arbitrary"),
        name="moe_experts",
    )(tile_e, slot_tok, n_used, h, wg, wu, wd)


def _combine_ln_kernel(dest_ref, y_hbm, h_ref, wt_ref, g_ref, b_ref, of_ref, ob_ref, ybuf, sem, *, tm, nsteps):
    t = pl.program_id(0)

    def gather(step, slot):
        base = step * (tm * TOP_K)
        for r in range(tm):
            for k in range(TOP_K):
                d = dest_ref[base + r * TOP_K + k]
                pltpu.make_async_copy(y_hbm.at[pl.ds(d, 1)], ybuf.at[slot, k, pl.ds(r, 1)], sem.at[slot]).start()

    def wait(slot):
        for k in range(TOP_K):
            pltpu.make_async_copy(y_hbm.at[pl.ds(0, tm)], ybuf.at[slot, k], sem.at[slot]).wait()

    @pl.when(t == 0)
    def _():
        gather(0, 0)

    slot = lax.rem(t, 2)
    wait(slot)
    gather(jnp.minimum(t + 1, nsteps - 1), 1 - slot)
    w0 = wt_ref[:, 0:1]
    w1 = wt_ref[:, 1:2]
    for c in range(0, h_ref.shape[1], LN_CHUNK):
        sl = slice(c, c + LN_CHUNK)
        of_ref[:, sl] = ALPHA * h_ref[:, sl] + (ybuf[slot, 0, :, sl] * w0 + ybuf[slot, 1, :, sl] * w1)
    _layer_norm_inplace(of_ref, g_ref, b_ref, ob_ref)

    @pl.when(t == nsteps - 1)
    def _():
        wait(1 - slot)


def moe_combine_ln(y, h, wts, dest, g, b, layer, tm=256):
    N = wts.shape[0]
    D = g.shape[2]
    tm = min(tm, N)
    nsteps = N // tm
    vec = pl.BlockSpec((None, 1, D), lambda i, d: (layer, 0, 0))
    grid_spec = pltpu.PrefetchScalarGridSpec(
        num_scalar_prefetch=1,
        grid=(nsteps,),
        in_specs=[pl.BlockSpec(memory_space=pl.ANY),
                  pl.BlockSpec((tm, D), lambda i, d: (i, 0)),
                  pl.BlockSpec((tm, LANES), lambda i, d: (i, 0)),
                  vec, vec],
        out_specs=[pl.BlockSpec((tm, D), lambda i, d: (i, 0)),
                   pl.BlockSpec((tm, D), lambda i, d: (i, 0))],
        scratch_shapes=[pltpu.VMEM((2, TOP_K, tm, D), F32), pltpu.SemaphoreType.DMA((2,))],
    )
    return pl.pallas_call(
        functools.partial(_combine_ln_kernel, tm=tm, nsteps=nsteps),
        grid_spec=grid_spec,
        out_shape=[jax.ShapeDtypeStruct((N, D), F32), jax.ShapeDtypeStruct((N, D), BF16)],
        compiler_params=_params("arbitrary"),
        name="moe_combine_ln",
    )(dest, y, h, wts, g, b)


def _dispatch_plan(e_idx):
    N = e_idx.shape[0]
    A = N * TOP_K
    T = EXPERT_TILE
    n_tiles = (A + N_EXPERTS * (T - 1) + T - 1) // T
    flat_e = e_idx.reshape(-1)
    onehot = jax.nn.one_hot(flat_e, N_EXPERTS, dtype=jnp.int32)
    counts = jnp.sum(onehot, axis=0)
    rank = jnp.sum((jnp.cumsum(onehot, axis=0) - 1) * onehot, axis=1)
    padded = (counts + T - 1) // T * T
    pad_end = jnp.cumsum(padded)
    dest = (pad_end[flat_e] - padded[flat_e] + rank).astype(jnp.int32)
    slot_tok = jnp.zeros((n_tiles * T,), jnp.int32).at[dest].set(jnp.arange(A, dtype=jnp.int32) // TOP_K)
    starts = jnp.arange(n_tiles, dtype=jnp.int32) * T
    tile_e = jnp.minimum(jnp.sum((pad_end[None, :] <= starts[:, None]).astype(jnp.int32), axis=1), N_EXPERTS - 1)
    n_used = (pad_end[-1] // T).astype(jnp.int32).reshape(1)
    return tile_e.astype(jnp.int32), slot_tok, n_used, dest


def _rotary_table(positions):
    inv_freq = ROPE_THETA ** (-jnp.arange(0, ROT_DIM, 2, dtype=F32) / ROT_DIM)
    ang = positions.astype(F32).reshape(-1)[:, None] * inv_freq
    cos, sin = jnp.cos(ang), jnp.sin(ang)
    n = ang.shape[0]
    rest = HEAD_DIM - ROT_DIM
    z = lambda w: jnp.zeros((n, w), F32)
    c = jnp.concatenate([cos, cos, jnp.ones((n, rest), F32)], axis=1)
    s_lo = jnp.concatenate([-sin, z(HEAD_DIM - ROT_HALF)], axis=1)
    s_hi = jnp.concatenate([z(ROT_HALF), sin, z(rest)], axis=1)
    rep = LANES // HEAD_DIM
    return jnp.concatenate([jnp.tile(c, (1, rep)), jnp.tile(s_lo, (1, rep)), jnp.tile(s_hi, (1, rep))], axis=1)


def kernel(x, positions, even_w_in, even_w_out, attn_sinks, hgrn_lb_logits, hgrn_gnorm_w, rec_w_in, rec_conv_w, rec_conv_b, rec_gate_a_w, rec_gate_a_b, rec_gate_x_w, rec_gate_x_b, rec_lambda, rec_w_out, ln_mix_g, ln_mix_b, ln_ffn_g, ln_ffn_b, router_group_w, router_group_b, router_expert_w, router_expert_b, moe_w_gate, moe_w_up, moe_w_down):
    B, S, D = x.shape
    N = B * S
    tab = _rotary_table(positions)
    sm = jax.nn.softmax(hgrn_lb_logits.astype(F32), axis=0)
    lb_table = (jnp.cumsum(sm, axis=0) - sm[:1])[:, None, :]
    pad = LANES - N_GROUPS - N_EXPERTS
    router_w = jnp.concatenate([router_group_w, router_expert_w, jnp.zeros((DEPTH, D, pad), F32)], axis=-1)
    router_b = jnp.concatenate([router_group_b, router_expert_b, jnp.zeros((DEPTH, pad), F32)], axis=-1)[:, None, :]
    router_w_hi = router_w.astype(BF16)
    router_w_lo = (router_w - router_w_hi.astype(F32)).astype(BF16)
    router_w2 = jnp.concatenate([router_w_hi, router_w_lo], axis=-1)
    row = lambda a: a[:, None, :]
    even_w_in_b, even_w_out_b = even_w_in.astype(BF16), even_w_out.astype(BF16)
    rec_w_in_b, rec_w_out_b = rec_w_in.astype(BF16), rec_w_out.astype(BF16)
    wa_b, wx_b = rec_gate_a_w.astype(BF16), rec_gate_x_w.astype(BF16)
    wg_b, wu_b, wd_b = moe_w_gate.astype(BF16), moe_w_up.astype(BF16), moe_w_down.astype(BF16)
    gnorm = row(hgrn_gnorm_w)
    ln_mix_g3, ln_mix_b3, ln_ffn_g3, ln_ffn_b3 = row(ln_mix_g), row(ln_mix_b), row(ln_ffn_g), row(ln_ffn_b)

    h = x.reshape(N, D)
    hb = h.astype(BF16)
    for layer in range(DEPTH):
        j = layer // 2
        if layer % 2 == 0:
            proj = matmul(hb, even_w_in_b, j, F32)
            o_a = swa_attention(proj, tab, attn_sinks[j], B, S)
            o_b = hgrn2(proj, lb_table, gnorm, j, B, S)
            mixed = jnp.concatenate([o_a, *o_b], axis=1)
            w_out = even_w_out_b
        else:
            proj = matmul(hb, rec_w_in_b, j, F32)
            mixed = recurrent_block(proj, rec_conv_w, row(rec_conv_b), wa_b, row(rec_gate_a_b), wx_b,
                                    row(rec_gate_x_b), row(rec_lambda), j, B, S)
            w_out = rec_w_out_b
        h, idx, wts = matmul_residual_ln_route(mixed, w_out, h, ln_mix_g3, ln_mix_b3, router_w2, router_b, layer)
        tile_e, slot_tok, n_used, dest = _dispatch_plan(idx[:, :TOP_K])
        y = moe_experts(h, wg_b, wu_b, wd_b, tile_e, slot_tok, n_used, layer)
        h, hb = moe_combine_ln(y, h, wts, dest, ln_ffn_g3, ln_ffn_b3, layer)
    return h.reshape(B, S, D)
```

```python
import functools

import jax
import jax.numpy as jnp
from jax import lax
from jax.experimental import pallas as pl
from jax.experimental.pallas import tpu as pltpu

F32 = jnp.float32
BF16 = jnp.bfloat16

D_MODEL = 4096
DEPTH = 4
ATTN_WIDTH = D_MODEL // 2
HEAD_DIM = 64
N_Q_HEADS = ATTN_WIDTH // HEAD_DIM
N_KV_HEADS = N_Q_HEADS // 8
Q_PER_KV = N_Q_HEADS // N_KV_HEADS
KV_WIDTH = N_KV_HEADS * HEAD_DIM
WINDOW = 128
ROT_DIM = HEAD_DIM // 4
ROT_HALF = ROT_DIM // 2
ROPE_THETA = 500000.0
MASK_VALUE = -1e9
HGRN_WIDTH = D_MODEL - ATTN_WIDTH
HGRN_DIM = 128
N_HGRN_HEADS = HGRN_WIDTH // HGRN_DIM
HGRN_CHUNK = 64
HGRN_SUB = 16
HGRN_HEADS_PER_BLOCK = 4
HGRN_BLOCKS_PER_STEP = 2
HGRN_PIVOT_MAX_DECAY = 75.0
LRU_WIDTH = D_MODEL
N_LRU_HEADS = 16
LRU_HEAD_DIM = LRU_WIDTH // N_LRU_HEADS
CONV_WIDTH = 4
RG_C = 8.0
N_GROUPS = 4
EXPERTS_PER_GROUP = 8
N_EXPERTS = N_GROUPS * EXPERTS_PER_GROUP
TOP_K = 2
EXPERT_FF = 3 * D_MODEL // 32
EXPERT_TILE = 256
ALPHA = (2.0 * DEPTH) ** 0.25
LN_EPS = 1e-5
RMS_EPS = 1e-6

LANES = 128
SUBLANES = 8
VMEM_LIMIT = 56 * 1024 * 1024
NEG_BIG = -1e30

_NT = (((1,), (1,)), ((), ()))
_TN = (((0,), (0,)), ((), ()))


def _params(*sem):
    return pltpu.CompilerParams(dimension_semantics=sem, vmem_limit_bytes=VMEM_LIMIT)


def _mm_kernel(x_ref, w_ref, o_ref):
    o_ref[...] = jnp.dot(x_ref[...], w_ref[...], preferred_element_type=F32).astype(o_ref.dtype)


def matmul(x, w, layer, out_dtype, tm=1024, tn=512):
    M, K = x.shape
    N = w.shape[2]
    tm = min(tm, M)
    return pl.pallas_call(
        _mm_kernel,
        grid=(M // tm, N // tn),
        in_specs=[pl.BlockSpec((tm, K), lambda i, j: (i, 0)),
                  pl.BlockSpec((None, K, tn), lambda i, j: (layer, 0, j))],
        out_specs=pl.BlockSpec((tm, tn), lambda i, j: (i, j)),
        out_shape=jax.ShapeDtypeStruct((M, N), out_dtype),
        compiler_params=_params("parallel", "arbitrary"),
        name="matmul",
    )(x, w)


LN_CHUNK = 512
HI_MASK = 0xFFFF0000


def _pack_bf16_pair(a, b):
    ua = lax.bitcast_convert_type(a.astype(BF16).astype(F32), jnp.uint32)
    ub = lax.bitcast_convert_type(b.astype(BF16).astype(F32), jnp.uint32)
    return (ua & jnp.uint32(HI_MASK)) | (ub >> 16)


def _unpack_bf16_pair(w):
    a = lax.bitcast_convert_type(w & jnp.uint32(HI_MASK), F32)
    b = lax.bitcast_convert_type(w << 16, F32)
    return a, b


def _layer_norm_inplace(of_ref, g_ref, b_ref, ob_ref=None, op_ref=None):
    D = of_ref.shape[1]
    cols = [slice(c, c + LN_CHUNK) for c in range(0, D, LN_CHUNK)]
    acc = None
    for sl in cols:
        acc = of_ref[:, sl] if acc is None else acc + of_ref[:, sl]
    mu = jnp.sum(acc, axis=-1, keepdims=True) * (1.0 / D)
    acc = None
    for sl in cols:
        d = of_ref[:, sl] - mu
        acc = d * d if acc is None else acc + d * d
    rstd = lax.rsqrt(jnp.sum(acc, axis=-1, keepdims=True) * (1.0 / D) + LN_EPS)
    for sl in cols:
        y = (of_ref[:, sl] - mu) * rstd * g_ref[:, sl] + b_ref[:, sl]
        of_ref[:, sl] = y
        if ob_ref is not None:
            ob_ref[:, sl] = y.astype(BF16)
    if op_ref is not None:
        for c in range(0, D // 2, LN_CHUNK):
            op_ref[:, c:c + LN_CHUNK] = _pack_bf16_pair(of_ref[:, c:c + LN_CHUNK],
                                                        of_ref[:, D // 2 + c:D // 2 + c + LN_CHUNK])


ROUTE_ROWS = 256


def _route(h, w_ref, b_ref):
    n = h.shape[0]
    h_hi = h.astype(BF16)
    h_lo = (h - h_hi.astype(F32)).astype(BF16)
    parts = jnp.dot(jnp.concatenate([h_hi, h_lo], axis=0), w_ref[...], preferred_element_type=F32)
    logits = (parts[:n, :LANES] + (parts[:n, LANES:] + parts[n:, :LANES]) + parts[n:, LANES:]) + b_ref[...]
    lane = lax.broadcasted_iota(jnp.int32, logits.shape, 1)
    neg = -jnp.inf
    big = jnp.int32(LANES)
    gl = jnp.where(lane < N_GROUPS, logits, neg)
    gmax = jnp.max(gl, axis=-1, keepdims=True)
    g_sel = jnp.min(jnp.where(gl == gmax, lane, big), axis=-1, keepdims=True)
    p_top = 1.0 / jnp.sum(jnp.exp(gl - gmax), axis=-1, keepdims=True)
    lo = N_GROUPS + g_sel * EXPERTS_PER_GROUP
    el = jnp.where((lane >= lo) & (lane < lo + EXPERTS_PER_GROUP), logits, neg)
    m1 = jnp.max(el, axis=-1, keepdims=True)
    i1 = jnp.min(jnp.where(el == m1, lane, big), axis=-1, keepdims=True)
    el2 = jnp.where(lane == i1, neg, el)
    m2 = jnp.max(el2, axis=-1, keepdims=True)
    i2 = jnp.min(jnp.where(el2 == m2, lane, big), axis=-1, keepdims=True)
    z = jnp.sum(jnp.exp(el - m1), axis=-1, keepdims=True)
    p1 = 1.0 / z
    p2 = jnp.exp(m2 - m1) / z
    tot = p1 + p2
    last = N_EXPERTS - 1
    idx = jnp.where(lane == 0, jnp.minimum(i1 - N_GROUPS, last), jnp.where(lane == 1, jnp.minimum(i2 - N_GROUPS, last), 0))
    wt = jnp.where(lane == 0, p1 / tot * p_top, jnp.where(lane == 1, p2 / tot * p_top, 0.0))
    return idx, wt


def _mm_ln_route_kernel(x_ref, w_ref, res_ref, g_ref, b_ref, rw_ref, rb_ref, of_ref, op_ref, idx_ref, wt_ref,
                        *, nj, tn):
    j = pl.program_id(1)
    pre = ALPHA * res_ref[...] + jnp.dot(x_ref[...], w_ref[...], preferred_element_type=F32)
    for c in range(nj):
        @pl.when(j == c)
        def _(c=c):
            of_ref[:, c * tn:(c + 1) * tn] = pre

    @pl.when(j == nj - 1)
    def _():
        _layer_norm_inplace(of_ref, g_ref, b_ref, op_ref=op_ref)

        def route_rows(r, carry):
            rows = pl.ds(pl.multiple_of(r * ROUTE_ROWS, ROUTE_ROWS), ROUTE_ROWS)
            idx_ref[rows, :], wt_ref[rows, :] = _route(of_ref[rows, :], rw_ref, rb_ref)
            return carry

        lax.fori_loop(0, of_ref.shape[0] // ROUTE_ROWS, route_rows, 0)


def matmul_residual_ln_route(x, w, res, g, b, rw, rb, layer, tm=512, tn=512):
    M, K = x.shape
    N = w.shape[2]
    tm = min(tm, M)
    nj = N // tn
    vec = pl.BlockSpec((None, 1, N), lambda i, j: (layer, 0, 0))
    return pl.pallas_call(
        functools.partial(_mm_ln_route_kernel, nj=nj, tn=tn),
        grid=(M // tm, nj),
        in_specs=[pl.BlockSpec((tm, K), lambda i, j: (i, 0)),
                  pl.BlockSpec((None, K, tn), lambda i, j: (layer // 2, 0, j)),
                  pl.BlockSpec((tm, tn), lambda i, j: (i, j)),
                  vec, vec,
                  pl.BlockSpec((None, N, 2 * LANES), lambda i, j: (layer, 0, 0)),
                  pl.BlockSpec((None, 1, LANES), lambda i, j: (layer, 0, 0))],
        out_specs=[pl.BlockSpec((tm, N), lambda i, j: (i, 0)),
                   pl.BlockSpec((tm, N // 2), lambda i, j: (i, 0)),
                   pl.BlockSpec((tm, LANES), lambda i, j: (i, 0)),
                   pl.BlockSpec((tm, LANES), lambda i, j: (i, 0))],
        out_shape=[jax.ShapeDtypeStruct((M, N), F32),
                   jax.ShapeDtypeStruct((M, N // 2), jnp.uint32),
                   jax.ShapeDtypeStruct((M, LANES), jnp.int32),
                   jax.ShapeDtypeStruct((M, LANES), F32)],
        compiler_params=_params("parallel", "arbitrary"),
        name="matmul_residual_ln_route",
    )(x, w, res, g, b, rw, rb)


def _rotary(x, tab):
    c, s_lo, s_hi = tab[:, :LANES], tab[:, LANES:2 * LANES], tab[:, 2 * LANES:]
    out = []
    for i in range(x.shape[1] // LANES):
        xc = x[:, i * LANES:(i + 1) * LANES]
        out.append(xc * c + pltpu.roll(xc, LANES - ROT_HALF, 1) * s_lo + pltpu.roll(xc, ROT_HALF, 1) * s_hi)
    return out


def _attn_kernel(sink_ref, q_ref, kc_ref, vc_ref, kp_ref, vp_ref, tc_ref, tp_ref, mask_ref, o_ref):
    W = WINDOW
    tab_c = tc_ref[...]
    tab_p = tp_ref[...]
    q_cols = _rotary(q_ref[...], tab_c)
    kc = jnp.concatenate(_rotary(kc_ref[...], tab_c), axis=1)
    kp = jnp.concatenate(_rotary(kp_ref[...], tab_p), axis=1)
    k_all = jnp.concatenate([kp, kc], axis=0).astype(BF16)
    v_all = jnp.concatenate([vp_ref[...], vc_ref[...]], axis=0).astype(BF16)
    outs = []
    for kv in range(N_KV_HEADS):
        heads = range(kv * Q_PER_KV, (kv + 1) * Q_PER_KV)
        qg = jnp.concatenate(
            [q_cols[h // 2][:, (h % 2) * HEAD_DIM:(h % 2 + 1) * HEAD_DIM] for h in heads], axis=0).astype(BF16)
        kh = k_all[:, kv * HEAD_DIM:(kv + 1) * HEAD_DIM]
        vh = v_all[:, kv * HEAD_DIM:(kv + 1) * HEAD_DIM]
        s_all = lax.dot_general(qg, kh, _NT, preferred_element_type=F32)
        ps, denoms = [], []
        for g, h in enumerate(heads):
            s = s_all[g * W:(g + 1) * W] * mask_ref[:, :2 * W] + mask_ref[:, 2 * W:4 * W]
            sink = sink_ref[h]
            m = jnp.maximum(jnp.max(s, axis=-1, keepdims=True), sink)
            p = jnp.exp(s - m) * mask_ref[:, 4 * W:]
            denoms.append(jnp.sum(p, axis=-1, keepdims=True) + jnp.exp(sink - m))
            ps.append(p.astype(BF16))
        o_all = jnp.dot(jnp.concatenate(ps, axis=0), vh, preferred_element_type=F32)
        for g in range(Q_PER_KV):
            outs.append(o_all[g * W:(g + 1) * W] / denoms[g])
    o_ref[...] = jnp.concatenate(outs, axis=1).astype(o_ref.dtype)


def _band_masks():
    qi = jnp.arange(WINDOW)[:, None]
    kj = jnp.arange(2 * WINDOW)[None, :]
    rel = qi + WINDOW - kj
    in_band = (rel >= 0) & (rel < WINDOW)
    out = []
    for has_prev in (False, True):
        keep = (in_band & ((kj >= WINDOW) | has_prev)).astype(F32)
        out.append(jnp.concatenate([keep * (HEAD_DIM ** -0.5), (1.0 - keep) * MASK_VALUE, keep], axis=1))
    return jnp.stack(out)


def swa_attention(proj, tab, sinks, batch, seq):
    nb = seq // WINDOW
    kcol = ATTN_WIDTH // KV_WIDTH
    cur = lambda b, n: b * nb + n
    prev = lambda b, n: b * nb + jnp.maximum(n - 1, 0)
    return pl.pallas_call(
        _attn_kernel,
        grid=(batch, nb),
        in_specs=[pl.BlockSpec(memory_space=pltpu.SMEM),
                  pl.BlockSpec((WINDOW, ATTN_WIDTH), lambda b, n: (cur(b, n), 0)),
                  pl.BlockSpec((WINDOW, KV_WIDTH), lambda b, n: (cur(b, n), kcol)),
                  pl.BlockSpec((WINDOW, KV_WIDTH), lambda b, n: (cur(b, n), kcol + 1)),
                  pl.BlockSpec((WINDOW, KV_WIDTH), lambda b, n: (prev(b, n), kcol)),
                  pl.BlockSpec((WINDOW, KV_WIDTH), lambda b, n: (prev(b, n), kcol + 1)),
                  pl.BlockSpec((WINDOW, 3 * LANES), lambda b, n: (cur(b, n), 0)),
                  pl.BlockSpec((WINDOW, 3 * LANES), lambda b, n: (prev(b, n), 0)),
                  pl.BlockSpec((None, WINDOW, 6 * WINDOW), lambda b, n: (jnp.minimum(n, 1), 0, 0))],
        out_specs=pl.BlockSpec((WINDOW, ATTN_WIDTH), lambda b, n: (cur(b, n), 0)),
        out_shape=jax.ShapeDtypeStruct((batch * seq, ATTN_WIDTH), BF16),
        compiler_params=_params("parallel", "parallel"),
        name="swa_attention",
    )(sinks, proj, proj, proj, proj, proj, tab, tab, _band_masks())


def _hgrn_gates(q_in, z, lb):
    C = HGRN_CHUNK
    sg = jax.nn.sigmoid(z)
    lf = jnp.log(lb + (1.0 - lb) * sg)
    k = (1.0 - lb) * jax.nn.sigmoid(-z)
    q = q_in * jax.nn.sigmoid(q_in)
    ti = lax.broadcasted_iota(jnp.int32, (C, C), 0)
    si = lax.broadcasted_iota(jnp.int32, (C, C), 1)
    tril = jnp.where(ti >= si, 1.0, 0.0).astype(F32)
    b = jnp.dot(tril, lf, preferred_element_type=F32, precision=lax.Precision.HIGHEST)
    return q, k, b


def _hgrn_intra_pivot(q, k, v, b):
    C = HGRN_CHUNK
    sc = lax.dot_general((q * jnp.exp(b)).astype(BF16), (k * jnp.exp(-b)).astype(BF16), _NT,
                         preferred_element_type=F32)
    ti = lax.broadcasted_iota(jnp.int32, (C, C), 0)
    si = lax.broadcasted_iota(jnp.int32, (C, C), 1)
    sc = jnp.where(ti >= si, sc, 0.0)
    return jnp.dot(sc.astype(BF16), v.astype(BF16), preferred_element_type=F32)


def _hgrn_intra_safe(q, k, v, b):
    C, c = HGRN_CHUNK, HGRN_SUB
    half = c // 2
    rows = lax.broadcasted_iota(jnp.int32, (half, HGRN_DIM), 0)
    pieces = [None] * (C // c)
    for j in range(C // c):
        lo, hi = j * c, (j + 1) * c
        bd, qd, kd, vd = b[lo:hi], q[lo:hi], k[lo:hi], v[lo:hi]
        e_j = b[hi - 1:hi, :]
        if hi < C:
            khat = (kd * jnp.exp(e_j - bd)).astype(BF16)
            qt = (q[hi:] * jnp.exp(b[hi:] - e_j)).astype(BF16)
            sc = lax.dot_general(qt, khat, _NT, preferred_element_type=F32)
            upd = jnp.dot(sc.astype(BF16), vd.astype(BF16), preferred_element_type=F32)
            for jj in range(j + 1, C // c):
                u = upd[(jj - j - 1) * c:(jj - j) * c]
                pieces[jj] = u if pieces[jj] is None else pieces[jj] + u
        acc = [jnp.zeros((half, HGRN_DIM), F32), jnp.zeros((half, HGRN_DIM), F32)]
        for s in range(c):
            for hf in range(s // half, 2):
                r0 = hf * half
                diff = bd[r0:r0 + half] - bd[s:s + 1]
                if s >= r0:
                    diff = jnp.where(rows >= s - r0, diff, NEG_BIG)
                col = jnp.sum(qd[r0:r0 + half] * kd[s:s + 1] * jnp.exp(diff), axis=-1, keepdims=True)
                acc[hf] = acc[hf] + col * vd[s:s + 1]
        d = jnp.concatenate(acc, axis=0)
        pieces[j] = d if pieces[j] is None else pieces[j] + d
    return jnp.concatenate(pieces, axis=0)


def _hgrn_finish(o_intra, q, k, v, b, g_in, gn, state_ref):
    C = HGRN_CHUNK
    b_last = b[C - 1:C, :]
    state = state_ref[...]
    o = o_intra + lax.dot_general((q * jnp.exp(b)).astype(BF16), state.astype(BF16), _NT,
                                  preferred_element_type=F32)
    kdec = (k * jnp.exp(b_last - b)).astype(BF16)
    state_ref[...] = state * jnp.exp(b_last) + lax.dot_general(v.astype(BF16), kdec, _TN, preferred_element_type=F32)
    ms = jnp.mean(o * o, axis=-1, keepdims=True)
    return o * lax.rsqrt(ms + RMS_EPS) * gn * (g_in * jax.nn.sigmoid(g_in))


def _hgrn_kernel(*refs, n_chunks):
    nb, hb = HGRN_BLOCKS_PER_STEP, HGRN_HEADS_PER_BLOCK
    q_refs, f_refs, i_refs, g_refs, lb_refs = (refs[k * nb:(k + 1) * nb] for k in range(5))
    gn_ref = refs[5 * nb]
    o_refs = refs[5 * nb + 1:6 * nb + 1]
    state_ref = refs[6 * nb + 1]

    @pl.when(pl.program_id(2) == 0)
    def _():
        state_ref[...] = jnp.zeros_like(state_ref)

    gn = gn_ref[...]
    heads = [(blk, slice(hd * HGRN_DIM, (hd + 1) * HGRN_DIM)) for blk in range(nb) for hd in range(hb)]

    def body(ci, carry):
        r0 = pl.multiple_of(ci * HGRN_CHUNK, HGRN_CHUNK)
        rs = pl.ds(r0, HGRN_CHUNK)
        gates = [_hgrn_gates(q_refs[blk][rs, cs], f_refs[blk][rs, cs], lb_refs[blk][:, cs]) for blk, cs in heads]
        total = gates[0][2][HGRN_CHUNK - 1:HGRN_CHUNK, :]
        for _, _, b in gates[1:]:
            total = jnp.minimum(total, b[HGRN_CHUNK - 1:HGRN_CHUNK, :])
        mild = jnp.min(total) >= -HGRN_PIVOT_MAX_DECAY

        def run(intra):
            for n, (blk, cs) in enumerate(heads):
                q, k, b = gates[n]
                v = i_refs[blk][rs, cs]
                y = _hgrn_finish(intra(q, k, v, b), q, k, v, b, g_refs[blk][rs, cs], gn, state_ref.at[n])
                o_refs[blk][rs, cs] = y.astype(o_refs[blk].dtype)

        pl.when(mild)(lambda: run(_hgrn_intra_pivot))
        pl.when(jnp.logical_not(mild))(lambda: run(_hgrn_intra_safe))
        return carry

    lax.fori_loop(0, n_chunks, body, 0)


def hgrn2(proj, lb, gnorm, layer, batch, seq, tile=512):
    tile = min(tile, seq)
    nt = seq // tile
    nb = HGRN_BLOCKS_PER_STEP
    w = HGRN_HEADS_PER_BLOCK * HGRN_DIM
    c0 = (ATTN_WIDTH + 2 * KV_WIDTH) // w
    per_kind = HGRN_WIDTH // w
    per_slab = per_kind // nb
    spec = lambda k, blk: pl.BlockSpec(
        (tile, w), lambda b, h, t: (b * nt + t, c0 + k * per_kind + blk * per_slab + h))
    blocks = range(nb)
    return pl.pallas_call(
        functools.partial(_hgrn_kernel, n_chunks=tile // HGRN_CHUNK),
        grid=(batch, per_slab, nt),
        in_specs=([spec(k, blk) for k in range(4) for blk in blocks]
                  + [pl.BlockSpec((None, 1, w), lambda b, h, t, blk=blk: (layer, 0, blk * per_slab + h))
                     for blk in blocks]
                  + [pl.BlockSpec((None, 1, HGRN_DIM), lambda b, h, t: (layer, 0, 0))]),
        out_specs=[pl.BlockSpec((tile, w), lambda b, h, t: (b * nt + t, h)) for _ in blocks],
        out_shape=[jax.ShapeDtypeStruct((batch * seq, HGRN_WIDTH // nb), BF16) for _ in blocks],
        scratch_shapes=[pltpu.VMEM((nb * HGRN_HEADS_PER_BLOCK, HGRN_DIM, HGRN_DIM), F32)],
        compiler_params=_params("parallel", "parallel", "arbitrary"),
        name="hgrn2",
    )(*([proj] * (4 * nb)), *([lb] * nb), gnorm)


def _lru_kernel(gate_ref, rnn_ref, cw_ref, cb_ref, wa_ref, wx_ref, ba_ref, bx_ref, lam_ref, o_ref,
                xs_ref, h_ref, *, tile):
    @pl.when(pl.program_id(2) == 0)
    def _():
        xs_ref[0:SUBLANES, :] = jnp.zeros((SUBLANES, LRU_HEAD_DIM), F32)
        h_ref[...] = jnp.zeros_like(h_ref)

    x = rnn_ref[...]
    xs_ref[SUBLANES:, :] = x
    u = cb_ref[...] + x * cw_ref[CONV_WIDTH - 1:CONV_WIDTH, :]
    for j in range(CONV_WIDTH - 1):
        back = CONV_WIDTH - 1 - j
        u = u + xs_ref[pl.ds(SUBLANES - back, tile), :] * cw_ref[j:j + 1, :]
    xs_ref[0:SUBLANES, :] = x[tile - SUBLANES:, :]

    ub = u.astype(BF16)
    r = jax.nn.sigmoid(jnp.dot(ub, wa_ref[...], preferred_element_type=F32) + ba_ref[...])
    ig = jax.nn.sigmoid(jnp.dot(ub, wx_ref[...], preferred_element_type=F32) + bx_ref[...])
    nl = -lam_ref[...]
    softplus = jnp.maximum(nl, 0.0) + jnp.log1p(jnp.exp(-jnp.abs(nl)))
    log_a = -RG_C * r * softplus
    a = jnp.exp(log_a)
    bb = jnp.sqrt(jnp.maximum(-jnp.tanh(log_a) * (a * a + 1.0), 0.0)) * (ig * u)

    rows = lax.broadcasted_iota(jnp.int32, (tile, LRU_HEAD_DIM), 0)
    k = 1
    while k < tile:
        keep = rows >= k
        a_sh = jnp.where(keep, pltpu.roll(a, k, 0), 1.0)
        b_sh = jnp.where(keep, pltpu.roll(bb, k, 0), 0.0)
        bb = a * b_sh + bb
        a = a * a_sh
        k *= 2
    h = a * h_ref[...] + bb
    h_ref[...] = h[tile - 1:tile, :]
    o_ref[...] = (h * jax.nn.gelu(gate_ref[...])).astype(o_ref.dtype)


def recurrent_block(proj, conv_w, conv_b, wa, ba, wx, bx, lam, layer, batch, seq, tile=256):
    tile = min(tile, seq)
    nt = seq // tile
    hd = LRU_HEAD_DIM
    vec = pl.BlockSpec((None, 1, hd), lambda b, h, t: (layer, 0, h))
    mat = pl.BlockSpec((None, None, hd, hd), lambda b, h, t: (layer, h, 0, 0))
    return pl.pallas_call(
        functools.partial(_lru_kernel, tile=tile),
        grid=(batch, N_LRU_HEADS, nt),
        in_specs=[pl.BlockSpec((tile, hd), lambda b, h, t: (b * nt + t, h)),
                  pl.BlockSpec((tile, hd), lambda b, h, t: (b * nt + t, N_LRU_HEADS + h)),
                  pl.BlockSpec((None, CONV_WIDTH, hd), lambda b, h, t: (layer, 0, h)),
                  vec, mat, mat, vec, vec, vec],
        out_specs=pl.BlockSpec((tile, hd), lambda b, h, t: (b * nt + t, h)),
        out_shape=jax.ShapeDtypeStruct((batch * seq, LRU_WIDTH), BF16),
        scratch_shapes=[pltpu.VMEM((tile + SUBLANES, hd), F32), pltpu.VMEM((1, hd), F32)],
        compiler_params=_params("parallel", "parallel", "arbitrary"),
        name="recurrent_block",
    )(proj, proj, conv_w, conv_b, wa, wx, ba, bx, lam)


def _expert_kernel(te_ref, tok_ref, nused_ref, h_hbm, wg_ref, wu_ref, wd_ref, y_ref, xbuf, sem):
    t = pl.program_id(0)
    nused = nused_ref[0]
    T = EXPERT_TILE

    def gather(tile, slot):
        base = tile * T
        for r in range(T):
            tok = tok_ref[base + r]
            pltpu.make_async_copy(h_hbm.at[pl.ds(tok, 1)], xbuf.at[slot, pl.ds(r, 1)], sem.at[slot]).start()

    def wait(slot):
        pltpu.make_async_copy(h_hbm.at[pl.ds(0, T)], xbuf.at[slot], sem.at[slot]).wait()

    @pl.when(t == 0)
    def _():
        gather(0, 0)

    slot = lax.rem(t, 2)

    @pl.when(t < nused)
    def _():
        wait(slot)
        gather(jnp.minimum(t + 1, nused - 1), 1 - slot)
        xa, xb = _unpack_bf16_pair(xbuf[slot])
        x = jnp.concatenate([xa.astype(BF16), xb.astype(BF16)], axis=1)
        gate = jnp.dot(x, wg_ref[...], preferred_element_type=F32)
        up = jnp.dot(x, wu_ref[...], preferred_element_type=F32)
        hid = (gate * jax.nn.sigmoid(gate) * up).astype(BF16)
        y = jnp.dot(hid, wd_ref[...], preferred_element_type=F32)
        half = y.shape[1] // 2
        y_ref[...] = _pack_bf16_pair(y[:, :half], y[:, half:])

        @pl.when(t == nused - 1)
        def _():
            wait(1 - slot)

    @pl.when(t >= nused)
    def _():
        y_ref[...] = jnp.zeros_like(y_ref)


def moe_experts(h, wg, wu, wd, tile_e, slot_tok, n_used, layer):
    n_tiles = tile_e.shape[0]
    T = EXPERT_TILE
    D = wg.shape[2]
    grid_spec = pltpu.PrefetchScalarGridSpec(
        num_scalar_prefetch=3,
        grid=(n_tiles,),
        in_specs=[pl.BlockSpec(memory_space=pl.ANY),
                  pl.BlockSpec((None, None, D, EXPERT_FF), lambda i, te, tok, nu: (layer, te[i], 0, 0)),
                  pl.BlockSpec((None, None, D, EXPERT_FF), lambda i, te, tok, nu: (layer, te[i], 0, 0)),
                  pl.BlockSpec((None, None, EXPERT_FF, D), lambda i, te, tok, nu: (layer, te[i], 0, 0))],
        out_specs=pl.BlockSpec((T, D // 2), lambda i, te, tok, nu: (i, 0)),
        scratch_shapes=[pltpu.VMEM((2, T, D // 2), jnp.uint32), pltpu.SemaphoreType.DMA((2,))],
    )
    return pl.pallas_call(
        _expert_kernel,
        grid_spec=grid_spec,
        out_shape=jax.ShapeDtypeStruct((n_tiles * T, D // 2), jnp.uint32),
        compiler_params=_params("arbitrary"),
        name="moe_experts",
    )(tile_e, slot_tok, n_used, h, wg, wu, wd)


def _combine_ln_kernel(dest_ref, y_hbm, h_ref, wt_ref, g_ref, b_ref, of_ref, ob_ref, ybuf, sem, *, tm, nsteps):
    t = pl.program_id(0)

    def gather(step, slot):
        base = step * (tm * TOP_K)
        for r in range(tm):
            for k in range(TOP_K):
                d = dest_ref[base + r * TOP_K + k]
                pltpu.make_async_copy(y_hbm.at[pl.ds(d, 1)], ybuf.at[slot, k, pl.ds(r, 1)], sem.at[slot]).start(priority=k)

    def wait(slot):
        for k in range(TOP_K):
            pltpu.make_async_copy(y_hbm.at[pl.ds(0, tm)], ybuf.at[slot, k], sem.at[slot]).wait()

    @pl.when(t == 0)
    def _():
        gather(0, 0)

    slot = lax.rem(t, 2)
    wait(slot)
    gather(jnp.minimum(t + 1, nsteps - 1), 1 - slot)
    w0 = wt_ref[:, 0:1]
    w1 = wt_ref[:, 1:2]
    half = h_ref.shape[1] // 2
    for c in range(0, half, LN_CHUNK):
        sl_a, sl_b = slice(c, c + LN_CHUNK), slice(half + c, half + c + LN_CHUNK)
        y0a, y0b = _unpack_bf16_pair(ybuf[slot, 0, :, sl_a])
        y1a, y1b = _unpack_bf16_pair(ybuf[slot, 1, :, sl_a])
        of_ref[:, sl_a] = ALPHA * h_ref[:, sl_a] + (y0a * w0 + y1a * w1)
        of_ref[:, sl_b] = ALPHA * h_ref[:, sl_b] + (y0b * w0 + y1b * w1)
    _layer_norm_inplace(of_ref, g_ref, b_ref, ob_ref)

    @pl.when(t == nsteps - 1)
    def _():
        wait(1 - slot)


def moe_combine_ln(y, h, wts, dest, g, b, layer, tm=256):
    N = wts.shape[0]
    D = g.shape[2]
    tm = min(tm, N)
    nsteps = N // tm
    vec = pl.BlockSpec((None, 1, D), lambda i, d: (layer, 0, 0))
    grid_spec = pltpu.PrefetchScalarGridSpec(
        num_scalar_prefetch=1,
        grid=(nsteps,),
        in_specs=[pl.BlockSpec(memory_space=pl.ANY),
                  pl.BlockSpec((tm, D), lambda i, d: (i, 0)),
                  pl.BlockSpec((tm, LANES), lambda i, d: (i, 0)),
                  vec, vec],
        out_specs=[pl.BlockSpec((tm, D), lambda i, d: (i, 0)),
                   pl.BlockSpec((tm, D), lambda i, d: (i, 0))],
        scratch_shapes=[pltpu.VMEM((2, TOP_K, tm, D // 2), jnp.uint32), pltpu.SemaphoreType.DMA((2,))],
    )
    return pl.pallas_call(
        functools.partial(_combine_ln_kernel, tm=tm, nsteps=nsteps),
        grid_spec=grid_spec,
        out_shape=[jax.ShapeDtypeStruct((N, D), F32), jax.ShapeDtypeStruct((N, D), BF16)],
        compiler_params=_params("arbitrary"),
        name="moe_combine_ln",
    )(dest, y, h, wts, g, b)


def _dispatch_plan(e_idx):
    N = e_idx.shape[0]
    A = N * TOP_K
    T = EXPERT_TILE
    n_tiles = (A + N_EXPERTS * (T - 1) + T - 1) // T
    flat_e = e_idx.reshape(-1)
    onehot = jax.nn.one_hot(flat_e, N_EXPERTS, dtype=jnp.int32)
    counts = jnp.sum(onehot, axis=0)
    rank = jnp.sum((jnp.cumsum(onehot, axis=0) - 1) * onehot, axis=1)
    padded = (counts + T - 1) // T * T
    pad_end = jnp.cumsum(padded)
    dest = (pad_end[flat_e] - padded[flat_e] + rank).astype(jnp.int32)
    slot_tok = jnp.zeros((n_tiles * T,), jnp.int32).at[dest].set(jnp.arange(A, dtype=jnp.int32) // TOP_K)
    starts = jnp.arange(n_tiles, dtype=jnp.int32) * T
    tile_e = jnp.minimum(jnp.sum((pad_end[None, :] <= starts[:, None]).astype(jnp.int32), axis=1), N_EXPERTS - 1)
    n_used = (pad_end[-1] // T).astype(jnp.int32).reshape(1)
    return tile_e.astype(jnp.int32), slot_tok, n_used, dest


def _rotary_table(positions):
    inv_freq = ROPE_THETA ** (-jnp.arange(0, ROT_DIM, 2, dtype=F32) / ROT_DIM)
    ang = positions.astype(F32).reshape(-1)[:, None] * inv_freq
    cos, sin = jnp.cos(ang), jnp.sin(ang)
    n = ang.shape[0]
    rest = HEAD_DIM - ROT_DIM
    z = lambda w: jnp.zeros((n, w), F32)
    c = jnp.concatenate([cos, cos, jnp.ones((n, rest), F32)], axis=1)
    s_lo = jnp.concatenate([-sin, z(HEAD_DIM - ROT_HALF)], axis=1)
    s_hi = jnp.concatenate([z(ROT_HALF), sin, z(rest)], axis=1)
    rep = LANES // HEAD_DIM
    return jnp.concatenate([jnp.tile(c, (1, rep)), jnp.tile(s_lo, (1, rep)), jnp.tile(s_hi, (1, rep))], axis=1)


def kernel(x, positions, even_w_in, even_w_out, attn_sinks, hgrn_lb_logits, hgrn_gnorm_w, rec_w_in, rec_conv_w, rec_conv_b, rec_gate_a_w, rec_gate_a_b, rec_gate_x_w, rec_gate_x_b, rec_lambda, rec_w_out, ln_mix_g, ln_mix_b, ln_ffn_g, ln_ffn_b, router_group_w, router_group_b, router_expert_w, router_expert_b, moe_w_gate, moe_w_up, moe_w_down):
    B, S, D = x.shape
    N = B * S
    tab = _rotary_table(positions)
    sm = jax.nn.softmax(hgrn_lb_logits.astype(F32), axis=0)
    lb_table = (jnp.cumsum(sm, axis=0) - sm[:1])[:, None, :]
    pad = LANES - N_GROUPS - N_EXPERTS
    router_w = jnp.concatenate([router_group_w, router_expert_w, jnp.zeros((DEPTH, D, pad), F32)], axis=-1)
    router_b = jnp.concatenate([router_group_b, router_expert_b, jnp.zeros((DEPTH, pad), F32)], axis=-1)[:, None, :]
    router_w_hi = router_w.astype(BF16)
    router_w_lo = (router_w - router_w_hi.astype(F32)).astype(BF16)
    router_w2 = jnp.concatenate([router_w_hi, router_w_lo], axis=-1)
    row = lambda a: a[:, None, :]
    even_w_in_b, even_w_out_b = even_w_in.astype(BF16), even_w_out.astype(BF16)
    rec_w_in_b, rec_w_out_b = rec_w_in.astype(BF16), rec_w_out.astype(BF16)
    wa_b, wx_b = rec_gate_a_w.astype(BF16), rec_gate_x_w.astype(BF16)
    wg_b, wu_b, wd_b = moe_w_gate.astype(BF16), moe_w_up.astype(BF16), moe_w_down.astype(BF16)
    gnorm = row(hgrn_gnorm_w)
    ln_mix_g3, ln_mix_b3, ln_ffn_g3, ln_ffn_b3 = row(ln_mix_g), row(ln_mix_b), row(ln_ffn_g), row(ln_ffn_b)

    h = x.reshape(N, D)
    hb = h.astype(BF16)
    for layer in range(DEPTH):
        j = layer // 2
        if layer % 2 == 0:
            proj = matmul(hb, even_w_in_b, j, F32)
            o_a = swa_attention(proj, tab, attn_sinks[j], B, S)
            o_b = hgrn2(proj, lb_table, gnorm, j, B, S)
            mixed = jnp.concatenate([o_a, *o_b], axis=1)
            w_out = even_w_out_b
        else:
            proj = matmul(hb, rec_w_in_b, j, F32)
            mixed = recurrent_block(proj, rec_conv_w, row(rec_conv_b), wa_b, row(rec_gate_a_b), wx_b,
                                    row(rec_gate_x_b), row(rec_lambda), j, B, S)
            w_out = rec_w_out_b
        h, hp, idx, wts = matmul_residual_ln_route(mixed, w_out, h, ln_mix_g3, ln_mix_b3, router_w2, router_b, layer)
        tile_e, slot_tok, n_used, dest = _dispatch_plan(idx[:, :TOP_K])
        y = moe_experts(hp, wg_b, wu_b, wd_b, tile_e, slot_tok, n_used, layer)
        h, hb = moe_combine_ln(y, h, wts, dest, ln_ffn_g3, ln_ffn_b3, layer)
    return h.reshape(B, S, D)
```

```python
import functools

import jax
import jax.numpy as jnp
from jax import lax
from jax.experimental import pallas as pl
from jax.experimental.pallas import tpu as pltpu

F32 = jnp.float32
BF16 = jnp.bfloat16

D_MODEL = 4096
DEPTH = 4
ATTN_WIDTH = D_MODEL // 2
HEAD_DIM = 64
N_Q_HEADS = ATTN_WIDTH // HEAD_DIM
N_KV_HEADS = N_Q_HEADS // 8
Q_PER_KV = N_Q_HEADS // N_KV_HEADS
KV_WIDTH = N_KV_HEADS * HEAD_DIM
WINDOW = 128
ROT_DIM = HEAD_DIM // 4
ROT_HALF = ROT_DIM // 2
ROPE_THETA = 500000.0
MASK_VALUE = -1e9
HGRN_WIDTH = D_MODEL - ATTN_WIDTH
HGRN_DIM = 128
N_HGRN_HEADS = HGRN_WIDTH // HGRN_DIM
HGRN_CHUNK = 64
HGRN_SUB = 16
HGRN_HEADS_PER_BLOCK = 4
HGRN_BLOCKS_PER_STEP = 2
HGRN_PIVOT_MAX_DECAY = 75.0
LRU_WIDTH = D_MODEL
N_LRU_HEADS = 16
LRU_HEAD_DIM = LRU_WIDTH // N_LRU_HEADS
CONV_WIDTH = 4
RG_C = 8.0
N_GROUPS = 4
EXPERTS_PER_GROUP = 8
N_EXPERTS = N_GROUPS * EXPERTS_PER_GROUP
TOP_K = 2
EXPERT_FF = 3 * D_MODEL // 32
EXPERT_TILE = 256
ALPHA = (2.0 * DEPTH) ** 0.25
LN_EPS = 1e-5
RMS_EPS = 1e-6

LANES = 128
SUBLANES = 8
VMEM_LIMIT = 56 * 1024 * 1024
NEG_BIG = -1e30

_NT = (((1,), (1,)), ((), ()))
_TN = (((0,), (0,)), ((), ()))


def _params(*sem):
    return pltpu.CompilerParams(dimension_semantics=sem, vmem_limit_bytes=VMEM_LIMIT)


def _mm_kernel(x_ref, w_ref, o_ref):
    o_ref[...] = jnp.dot(x_ref[...], w_ref[...].astype(BF16), preferred_element_type=F32).astype(o_ref.dtype)


def matmul(x, w, layer, out_dtype, tm=1024, tn=512):
    M, K = x.shape
    N = w.shape[2]
    tm = min(tm, M)
    return pl.pallas_call(
        _mm_kernel,
        grid=(M // tm, N // tn),
        in_specs=[pl.BlockSpec((tm, K), lambda i, j: (i, 0)),
                  pl.BlockSpec((None, K, tn), lambda i, j: (layer, 0, j))],
        out_specs=pl.BlockSpec((tm, tn), lambda i, j: (i, j)),
        out_shape=jax.ShapeDtypeStruct((M, N), out_dtype),
        compiler_params=_params("parallel", "arbitrary"),
        name="matmul",
    )(x, w)


LN_CHUNK = 512
HI_MASK = 0xFFFF0000


def _pack_bf16_pair(a, b):
    ua = lax.bitcast_convert_type(a.astype(BF16).astype(F32), jnp.uint32)
    ub = lax.bitcast_convert_type(b.astype(BF16).astype(F32), jnp.uint32)
    return (ua & jnp.uint32(HI_MASK)) | (ub >> 16)


def _unpack_bf16_pair(w):
    a = lax.bitcast_convert_type(w & jnp.uint32(HI_MASK), F32)
    b = lax.bitcast_convert_type(w << 16, F32)
    return a, b


def _layer_norm_inplace(of_ref, g_ref, b_ref, ob_ref=None, op_ref=None):
    D = of_ref.shape[1]
    assert D % (2 * LN_CHUNK) == 0
    cols = [slice(c, c + LN_CHUNK) for c in range(0, D, LN_CHUNK)]
    acc = None
    for sl in cols:
        acc = of_ref[:, sl] if acc is None else acc + of_ref[:, sl]
    mu = jnp.sum(acc, axis=-1, keepdims=True) * (1.0 / D)
    acc = None
    for sl in cols:
        d = of_ref[:, sl] - mu
        acc = d * d if acc is None else acc + d * d
    rstd = lax.rsqrt(jnp.sum(acc, axis=-1, keepdims=True) * (1.0 / D) + LN_EPS)
    def normalize(sl):
        y = (of_ref[:, sl] - mu) * rstd * g_ref[:, sl] + b_ref[:, sl]
        of_ref[:, sl] = y
        if ob_ref is not None:
            ob_ref[:, sl] = y.astype(BF16)
        return y

    for c in range(0, D // 2, LN_CHUNK):
        ya = normalize(slice(c, c + LN_CHUNK))
        yb = normalize(slice(D // 2 + c, D // 2 + c + LN_CHUNK))
        if op_ref is not None:
            op_ref[:, c:c + LN_CHUNK] = _pack_bf16_pair(ya, yb)


ROUTE_ROWS = 256


def _route(h, w_ref, b_ref):
    n = h.shape[0]
    h_hi = h.astype(BF16)
    h_lo = (h - h_hi.astype(F32)).astype(BF16)
    parts = jnp.dot(jnp.concatenate([h_hi, h_lo], axis=0), w_ref[...], preferred_element_type=F32)
    logits = (parts[:n, :LANES] + (parts[:n, LANES:] + parts[n:, :LANES]) + parts[n:, LANES:]) + b_ref[...]
    lane = lax.broadcasted_iota(jnp.int32, logits.shape, 1)
    neg = -jnp.inf
    big = jnp.int32(LANES)
    gl = jnp.where(lane < N_GROUPS, logits, neg)
    gmax = jnp.max(gl, axis=-1, keepdims=True)
    g_sel = jnp.min(jnp.where(gl == gmax, lane, big), axis=-1, keepdims=True)
    p_top = 1.0 / jnp.sum(jnp.exp(gl - gmax), axis=-1, keepdims=True)
    lo = N_GROUPS + g_sel * EXPERTS_PER_GROUP
    el = jnp.where((lane >= lo) & (lane < lo + EXPERTS_PER_GROUP), logits, neg)
    m1 = jnp.max(el, axis=-1, keepdims=True)
    i1 = jnp.min(jnp.where(el == m1, lane, big), axis=-1, keepdims=True)
    el2 = jnp.where(lane == i1, neg, el)
    m2 = jnp.max(el2, axis=-1, keepdims=True)
    i2 = jnp.min(jnp.where(el2 == m2, lane, big), axis=-1, keepdims=True)
    z = jnp.sum(jnp.exp(el - m1), axis=-1, keepdims=True)
    p1 = 1.0 / z
    p2 = jnp.exp(m2 - m1) / z
    tot = p1 + p2
    last = N_EXPERTS - 1
    idx = jnp.where(lane == 0, jnp.minimum(i1 - N_GROUPS, last), jnp.where(lane == 1, jnp.minimum(i2 - N_GROUPS, last), 0))
    wt = jnp.where(lane == 0, p1 / tot * p_top, jnp.where(lane == 1, p2 / tot * p_top, 0.0))
    return idx, wt


def _mm_ln_route_kernel(x_ref, w_ref, res_ref, g_ref, b_ref, rw_ref, rb_ref, of_ref, op_ref, idx_ref, wt_ref,
                        *, nj, tn):
    j = pl.program_id(1)
    pre = ALPHA * res_ref[...] + jnp.dot(x_ref[...], w_ref[...], preferred_element_type=F32)
    for c in range(nj):
        @pl.when(j == c)
        def _(c=c):
            of_ref[:, c * tn:(c + 1) * tn] = pre

    @pl.when(j == nj - 1)
    def _():
        _layer_norm_inplace(of_ref, g_ref, b_ref, op_ref=op_ref)

        def route_rows(r, carry):
            rows = pl.ds(pl.multiple_of(r * ROUTE_ROWS, ROUTE_ROWS), ROUTE_ROWS)
            idx_ref[rows, :], wt_ref[rows, :] = _route(of_ref[rows, :], rw_ref, rb_ref)
            return carry

        lax.fori_loop(0, of_ref.shape[0] // ROUTE_ROWS, route_rows, 0)


def matmul_residual_ln_route(x, w, res, g, b, rw, rb, layer, tm=512, tn=512):
    M, K = x.shape
    N = w.shape[2]
    tm = min(tm, M)
    nj = N // tn
    vec = pl.BlockSpec((None, 1, N), lambda i, j: (layer, 0, 0))
    return pl.pallas_call(
        functools.partial(_mm_ln_route_kernel, nj=nj, tn=tn),
        grid=(M // tm, nj),
        in_specs=[pl.BlockSpec((tm, K), lambda i, j: (i, 0)),
                  pl.BlockSpec((None, K, tn), lambda i, j: (layer // 2, 0, j)),
                  pl.BlockSpec((tm, tn), lambda i, j: (i, j)),
                  vec, vec,
                  pl.BlockSpec((None, N, 2 * LANES), lambda i, j: (layer, 0, 0)),
                  pl.BlockSpec((None, 1, LANES), lambda i, j: (layer, 0, 0))],
        out_specs=[pl.BlockSpec((tm, N), lambda i, j: (i, 0)),
                   pl.BlockSpec((tm, N // 2), lambda i, j: (i, 0)),
                   pl.BlockSpec((tm, LANES), lambda i, j: (i, 0)),
                   pl.BlockSpec((tm, LANES), lambda i, j: (i, 0))],
        out_shape=[jax.ShapeDtypeStruct((M, N), F32),
                   jax.ShapeDtypeStruct((M, N // 2), jnp.uint32),
                   jax.ShapeDtypeStruct((M, LANES), jnp.int32),
                   jax.ShapeDtypeStruct((M, LANES), F32)],
        compiler_params=_params("parallel", "arbitrary"),
        name="matmul_residual_ln_route",
    )(x, w, res, g, b, rw, rb)


def _rotary(x, tab):
    c, s_lo, s_hi = tab[:, :LANES], tab[:, LANES:2 * LANES], tab[:, 2 * LANES:]
    out = []
    for i in range(x.shape[1] // LANES):
        xc = x[:, i * LANES:(i + 1) * LANES]
        out.append(xc * c + pltpu.roll(xc, LANES - ROT_HALF, 1) * s_lo + pltpu.roll(xc, ROT_HALF, 1) * s_hi)
    return out


def _attn_kernel(sink_ref, q_ref, kc_ref, vc_ref, kp_ref, vp_ref, tc_ref, tp_ref, mask_ref, o_ref):
    W = WINDOW
    tab_c = tc_ref[...]
    tab_p = tp_ref[...]
    q_cols = _rotary(q_ref[...], tab_c)
    kc = jnp.concatenate(_rotary(kc_ref[...], tab_c), axis=1)
    kp = jnp.concatenate(_rotary(kp_ref[...], tab_p), axis=1)
    k_all = jnp.concatenate([kp, kc], axis=0).astype(BF16)
    v_all = jnp.concatenate([vp_ref[...], vc_ref[...]], axis=0).astype(BF16)
    outs = []
    for kv in range(N_KV_HEADS):
        heads = range(kv * Q_PER_KV, (kv + 1) * Q_PER_KV)
        qg = jnp.concatenate(
            [q_cols[h // 2][:, (h % 2) * HEAD_DIM:(h % 2 + 1) * HEAD_DIM] for h in heads], axis=0).astype(BF16)
        kh = k_all[:, kv * HEAD_DIM:(kv + 1) * HEAD_DIM]
        vh = v_all[:, kv * HEAD_DIM:(kv + 1) * HEAD_DIM]
        s_all = lax.dot_general(qg, kh, _NT, preferred_element_type=F32)
        ps, denoms = [], []
        for g, h in enumerate(heads):
            s = s_all[g * W:(g + 1) * W] * mask_ref[:, :2 * W] + mask_ref[:, 2 * W:4 * W]
            sink = sink_ref[h]
            m = jnp.maximum(jnp.max(s, axis=-1, keepdims=True), sink)
            p = jnp.exp(s - m) * mask_ref[:, 4 * W:]
            denoms.append(jnp.sum(p, axis=-1, keepdims=True) + jnp.exp(sink - m))
            ps.append(p.astype(BF16))
        o_all = jnp.dot(jnp.concatenate(ps, axis=0), vh, preferred_element_type=F32)
        for g in range(Q_PER_KV):
            outs.append(o_all[g * W:(g + 1) * W] / denoms[g])
    o_ref[...] = jnp.concatenate(outs, axis=1).astype(o_ref.dtype)


def _band_masks():
    qi = jnp.arange(WINDOW)[:, None]
    kj = jnp.arange(2 * WINDOW)[None, :]
    rel = qi + WINDOW - kj
    in_band = (rel >= 0) & (rel < WINDOW)
    out = []
    for has_prev in (False, True):
        keep = (in_band & ((kj >= WINDOW) | has_prev)).astype(F32)
        out.append(jnp.concatenate([keep * (HEAD_DIM ** -0.5), (1.0 - keep) * MASK_VALUE, keep], axis=1))
    return jnp.stack(out)


def swa_attention(proj, tab, sinks, batch, seq):
    nb = seq // WINDOW
    kcol = ATTN_WIDTH // KV_WIDTH
    cur = lambda b, n: b * nb + n
    prev = lambda b, n: b * nb + jnp.maximum(n - 1, 0)
    return pl.pallas_call(
        _attn_kernel,
        grid=(batch, nb),
        in_specs=[pl.BlockSpec(memory_space=pltpu.SMEM),
                  pl.BlockSpec((WINDOW, ATTN_WIDTH), lambda b, n: (cur(b, n), 0)),
                  pl.BlockSpec((WINDOW, KV_WIDTH), lambda b, n: (cur(b, n), kcol)),
                  pl.BlockSpec((WINDOW, KV_WIDTH), lambda b, n: (cur(b, n), kcol + 1)),
                  pl.BlockSpec((WINDOW, KV_WIDTH), lambda b, n: (prev(b, n), kcol)),
                  pl.BlockSpec((WINDOW, KV_WIDTH), lambda b, n: (prev(b, n), kcol + 1)),
                  pl.BlockSpec((WINDOW, 3 * LANES), lambda b, n: (cur(b, n), 0)),
                  pl.BlockSpec((WINDOW, 3 * LANES), lambda b, n: (prev(b, n), 0)),
                  pl.BlockSpec((None, WINDOW, 6 * WINDOW), lambda b, n: (jnp.minimum(n, 1), 0, 0))],
        out_specs=pl.BlockSpec((WINDOW, ATTN_WIDTH), lambda b, n: (cur(b, n), 0)),
        out_shape=jax.ShapeDtypeStruct((batch * seq, ATTN_WIDTH), BF16),
        compiler_params=_params("parallel", "parallel"),
        name="swa_attention",
    )(sinks, proj, proj, proj, proj, proj, tab, tab, _band_masks())


def _hgrn_gates(q_in, z, lb):
    C = HGRN_CHUNK
    sg = jax.nn.sigmoid(z)
    lf = jnp.log(lb + (1.0 - lb) * sg)
    k = (1.0 - lb) * jax.nn.sigmoid(-z)
    q = q_in * jax.nn.sigmoid(q_in)
    ti = lax.broadcasted_iota(jnp.int32, (C, C), 0)
    si = lax.broadcasted_iota(jnp.int32, (C, C), 1)
    tril = jnp.where(ti >= si, 1.0, 0.0).astype(F32)
    b = jnp.dot(tril, lf, preferred_element_type=F32, precision=lax.Precision.HIGHEST)
    return q, k, b


def _hgrn_intra_pivot(q, k, v, b):
    C = HGRN_CHUNK
    sc = lax.dot_general((q * jnp.exp(b)).astype(BF16), (k * jnp.exp(-b)).astype(BF16), _NT,
                         preferred_element_type=F32)
    ti = lax.broadcasted_iota(jnp.int32, (C, C), 0)
    si = lax.broadcasted_iota(jnp.int32, (C, C), 1)
    sc = jnp.where(ti >= si, sc, 0.0)
    return jnp.dot(sc.astype(BF16), v.astype(BF16), preferred_element_type=F32)


def _hgrn_intra_safe(q, k, v, b):
    C, c = HGRN_CHUNK, HGRN_SUB
    half = c // 2
    rows = lax.broadcasted_iota(jnp.int32, (half, HGRN_DIM), 0)
    pieces = [None] * (C // c)
    for j in range(C // c):
        lo, hi = j * c, (j + 1) * c
        bd, qd, kd, vd = b[lo:hi], q[lo:hi], k[lo:hi], v[lo:hi]
        e_j = b[hi - 1:hi, :]
        if hi < C:
            khat = (kd * jnp.exp(e_j - bd)).astype(BF16)
            qt = (q[hi:] * jnp.exp(b[hi:] - e_j)).astype(BF16)
            sc = lax.dot_general(qt, khat, _NT, preferred_element_type=F32)
            upd = jnp.dot(sc.astype(BF16), vd.astype(BF16), preferred_element_type=F32)
            for jj in range(j + 1, C // c):
                u = upd[(jj - j - 1) * c:(jj - j) * c]
                pieces[jj] = u if pieces[jj] is None else pieces[jj] + u
        acc = [jnp.zeros((half, HGRN_DIM), F32), jnp.zeros((half, HGRN_DIM), F32)]
        for s in range(c):
            for hf in range(s // half, 2):
                r0 = hf * half
                diff = bd[r0:r0 + half] - bd[s:s + 1]
                if s >= r0:
                    diff = jnp.where(rows >= s - r0, diff, NEG_BIG)
                col = jnp.sum(qd[r0:r0 + half] * kd[s:s + 1] * jnp.exp(diff), axis=-1, keepdims=True)
                acc[hf] = acc[hf] + col * vd[s:s + 1]
        d = jnp.concatenate(acc, axis=0)
        pieces[j] = d if pieces[j] is None else pieces[j] + d
    return jnp.concatenate(pieces, axis=0)


def _hgrn_finish(o_intra, q, k, v, b, g_in, gn, state_ref):
    C = HGRN_CHUNK
    b_last = b[C - 1:C, :]
    state = state_ref[...]
    o = o_intra + lax.dot_general((q * jnp.exp(b)).astype(BF16), state.astype(BF16), _NT,
                                  preferred_element_type=F32)
    kdec = (k * jnp.exp(b_last - b)).astype(BF16)
    state_ref[...] = state * jnp.exp(b_last) + lax.dot_general(v.astype(BF16), kdec, _TN, preferred_element_type=F32)
    ms = jnp.mean(o * o, axis=-1, keepdims=True)
    return o * lax.rsqrt(ms + RMS_EPS) * gn * (g_in * jax.nn.sigmoid(g_in))


def _hgrn_kernel(*refs, n_chunks):
    nb, hb = HGRN_BLOCKS_PER_STEP, HGRN_HEADS_PER_BLOCK
    q_refs, f_refs, i_refs, g_refs, lb_refs = (refs[k * nb:(k + 1) * nb] for k in range(5))
    gn_ref = refs[5 * nb]
    o_refs = refs[5 * nb + 1:6 * nb + 1]
    state_ref = refs[6 * nb + 1]

    @pl.when(pl.program_id(2) == 0)
    def _():
        state_ref[...] = jnp.zeros_like(state_ref)

    gn = gn_ref[...]
    heads = [(blk, slice(hd * HGRN_DIM, (hd + 1) * HGRN_DIM)) for blk in range(nb) for hd in range(hb)]

    def body(ci, carry):
        r0 = pl.multiple_of(ci * HGRN_CHUNK, HGRN_CHUNK)
        rs = pl.ds(r0, HGRN_CHUNK)
        gates = [_hgrn_gates(q_refs[blk][rs, cs], f_refs[blk][rs, cs], lb_refs[blk][:, cs]) for blk, cs in heads]
        total = gates[0][2][HGRN_CHUNK - 1:HGRN_CHUNK, :]
        for _, _, b in gates[1:]:
            total = jnp.minimum(total, b[HGRN_CHUNK - 1:HGRN_CHUNK, :])
        mild = jnp.min(total) >= -HGRN_PIVOT_MAX_DECAY

        def run(intra):
            for n, (blk, cs) in enumerate(heads):
                q, k, b = gates[n]
                v = i_refs[blk][rs, cs]
                y = _hgrn_finish(intra(q, k, v, b), q, k, v, b, g_refs[blk][rs, cs], gn, state_ref.at[n])
                o_refs[blk][rs, cs] = y.astype(o_refs[blk].dtype)

        pl.when(mild)(lambda: run(_hgrn_intra_pivot))
        pl.when(jnp.logical_not(mild))(lambda: run(_hgrn_intra_safe))
        return carry

    lax.fori_loop(0, n_chunks, body, 0)


def hgrn2(proj, lb, gnorm, layer, batch, seq, tile=512):
    tile = min(tile, seq)
    nt = seq // tile
    nb = HGRN_BLOCKS_PER_STEP
    w = HGRN_HEADS_PER_BLOCK * HGRN_DIM
    c0 = (ATTN_WIDTH + 2 * KV_WIDTH) // w
    per_kind = HGRN_WIDTH // w
    per_slab = per_kind // nb
    spec = lambda k, blk: pl.BlockSpec(
        (tile, w), lambda b, h, t: (b * nt + t, c0 + k * per_kind + blk * per_slab + h))
    blocks = range(nb)
    return pl.pallas_call(
        functools.partial(_hgrn_kernel, n_chunks=tile // HGRN_CHUNK),
        grid=(batch, per_slab, nt),
        in_specs=([spec(k, blk) for k in range(4) for blk in blocks]
                  + [pl.BlockSpec((None, 1, w), lambda b, h, t, blk=blk: (layer, 0, blk * per_slab + h))
                     for blk in blocks]
                  + [pl.BlockSpec((None, 1, HGRN_DIM), lambda b, h, t: (layer, 0, 0))]),
        out_specs=[pl.BlockSpec((tile, w), lambda b, h, t: (b * nt + t, h)) for _ in blocks],
        out_shape=[jax.ShapeDtypeStruct((batch * seq, HGRN_WIDTH // nb), BF16) for _ in blocks],
        scratch_shapes=[pltpu.VMEM((nb * HGRN_HEADS_PER_BLOCK, HGRN_DIM, HGRN_DIM), F32)],
        compiler_params=_params("parallel", "parallel", "arbitrary"),
        name="hgrn2",
    )(*([proj] * (4 * nb)), *([lb] * nb), gnorm)


def _lru_kernel(gate_ref, rnn_ref, cw_ref, cb_ref, wa_ref, wx_ref, ba_ref, bx_ref, lam_ref, o_ref,
                xs_ref, h_ref, *, tile):
    @pl.when(pl.program_id(2) == 0)
    def _():
        xs_ref[0:SUBLANES, :] = jnp.zeros((SUBLANES, LRU_HEAD_DIM), F32)
        h_ref[...] = jnp.zeros_like(h_ref)

    x = rnn_ref[...]
    xs_ref[SUBLANES:, :] = x
    u = cb_ref[...] + x * cw_ref[CONV_WIDTH - 1:CONV_WIDTH, :]
    for j in range(CONV_WIDTH - 1):
        back = CONV_WIDTH - 1 - j
        u = u + xs_ref[pl.ds(SUBLANES - back, tile), :] * cw_ref[j:j + 1, :]
    xs_ref[0:SUBLANES, :] = x[tile - SUBLANES:, :]

    ub = u.astype(BF16)
    r = jax.nn.sigmoid(jnp.dot(ub, wa_ref[...], preferred_element_type=F32) + ba_ref[...])
    ig = jax.nn.sigmoid(jnp.dot(ub, wx_ref[...], preferred_element_type=F32) + bx_ref[...])
    nl = -lam_ref[...]
    softplus = jnp.maximum(nl, 0.0) + jnp.log1p(jnp.exp(-jnp.abs(nl)))
    log_a = -RG_C * r * softplus
    a = jnp.exp(log_a)
    bb = jnp.sqrt(jnp.maximum(-jnp.tanh(log_a) * (a * a + 1.0), 0.0)) * (ig * u)

    rows = lax.broadcasted_iota(jnp.int32, (tile, LRU_HEAD_DIM), 0)
    k = 1
    while k < tile:
        keep = rows >= k
        a_sh = jnp.where(keep, pltpu.roll(a, k, 0), 1.0)
        b_sh = jnp.where(keep, pltpu.roll(bb, k, 0), 0.0)
        bb = a * b_sh + bb
        a = a * a_sh
        k *= 2
    h = a * h_ref[...] + bb
    h_ref[...] = h[tile - 1:tile, :]
    o_ref[...] = (h * jax.nn.gelu(gate_ref[...])).astype(o_ref.dtype)


def recurrent_block(proj, conv_w, conv_b, wa, ba, wx, bx, lam, layer, batch, seq, tile=256):
    tile = min(tile, seq)
    nt = seq // tile
    hd = LRU_HEAD_DIM
    vec = pl.BlockSpec((None, 1, hd), lambda b, h, t: (layer, 0, h))
    mat = pl.BlockSpec((None, None, hd, hd), lambda b, h, t: (layer, h, 0, 0))
    return pl.pallas_call(
        functools.partial(_lru_kernel, tile=tile),
        grid=(batch, N_LRU_HEADS, nt),
        in_specs=[pl.BlockSpec((tile, hd), lambda b, h, t: (b * nt + t, h)),
                  pl.BlockSpec((tile, hd), lambda b, h, t: (b * nt + t, N_LRU_HEADS + h)),
                  pl.BlockSpec((None, CONV_WIDTH, hd), lambda b, h, t: (layer, 0, h)),
                  vec, mat, mat, vec, vec, vec],
        out_specs=pl.BlockSpec((tile, hd), lambda b, h, t: (b * nt + t, h)),
        out_shape=jax.ShapeDtypeStruct((batch * seq, LRU_WIDTH), BF16),
        scratch_shapes=[pltpu.VMEM((tile + SUBLANES, hd), F32), pltpu.VMEM((1, hd), F32)],
        compiler_params=_params("parallel", "parallel", "arbitrary"),
        name="recurrent_block",
    )(proj, proj, conv_w, conv_b, wa, wx, ba, bx, lam)


def _expert_kernel(te_ref, tok_ref, nused_ref, h_hbm, wg_ref, wu_ref, wd_ref, y_ref, xbuf, sem):
    t = pl.program_id(0)
    nused = nused_ref[0]
    T = EXPERT_TILE

    def gather(tile, slot):
        base = tile * T
        for r in range(T):
            tok = tok_ref[base + r]
            pltpu.make_async_copy(h_hbm.at[pl.ds(tok, 1)], xbuf.at[slot, pl.ds(r, 1)], sem.at[slot]).start()

    def wait(slot):
        pltpu.make_async_copy(h_hbm.at[pl.ds(0, T)], xbuf.at[slot], sem.at[slot]).wait()

    @pl.when(t == 0)
    def _():
        gather(0, 0)

    slot = lax.rem(t, 2)

    @pl.when(t < nused)
    def _():
        wait(slot)
        gather(jnp.minimum(t + 1, nused - 1), 1 - slot)
        xa, xb = _unpack_bf16_pair(xbuf[slot])
        x = jnp.concatenate([xa.astype(BF16), xb.astype(BF16)], axis=1)
        gate = jnp.dot(x, wg_ref[...].astype(BF16), preferred_element_type=F32)
        up = jnp.dot(x, wu_ref[...].astype(BF16), preferred_element_type=F32)
        hid = (gate * jax.nn.sigmoid(gate) * up).astype(BF16)
        y = jnp.dot(hid, wd_ref[...].astype(BF16), preferred_element_type=F32)
        half = y.shape[1] // 2
        y_ref[...] = _pack_bf16_pair(y[:, :half], y[:, half:])

        @pl.when(t == nused - 1)
        def _():
            wait(1 - slot)

    @pl.when(t >= nused)
    def _():
        y_ref[...] = jnp.zeros_like(y_ref)


def moe_experts(h, wg, wu, wd, tile_e, slot_tok, n_used, layer):
    n_tiles = tile_e.shape[0]
    T = EXPERT_TILE
    D = wg.shape[2]
    grid_spec = pltpu.PrefetchScalarGridSpec(
        num_scalar_prefetch=3,
        grid=(n_tiles,),
        in_specs=[pl.BlockSpec(memory_space=pl.ANY),
                  pl.BlockSpec((None, None, D, EXPERT_FF), lambda i, te, tok, nu: (layer, te[i], 0, 0)),
                  pl.BlockSpec((None, None, D, EXPERT_FF), lambda i, te, tok, nu: (layer, te[i], 0, 0)),
                  pl.BlockSpec((None, None, EXPERT_FF, D), lambda i, te, tok, nu: (layer, te[i], 0, 0))],
        out_specs=pl.BlockSpec((T, D // 2), lambda i, te, tok, nu: (i, 0)),
        scratch_shapes=[pltpu.VMEM((2, T, D // 2), jnp.uint32), pltpu.SemaphoreType.DMA((2,))],
    )
    return pl.pallas_call(
        _expert_kernel,
        grid_spec=grid_spec,
        out_shape=jax.ShapeDtypeStruct((n_tiles * T, D // 2), jnp.uint32),
        compiler_params=_params("arbitrary"),
        name="moe_experts",
    )(tile_e, slot_tok, n_used, h, wg, wu, wd)


def _combine_ln_kernel(dest_ref, y_hbm, h_ref, wt_ref, g_ref, b_ref, of_ref, ob_ref, ybuf, sem, *, tm, nsteps):
    t = pl.program_id(0)

    def gather(step, slot):
        base = step * (tm * TOP_K)
        for r in range(tm):
            for k in range(TOP_K):
                d = dest_ref[base + r * TOP_K + k]
                pltpu.make_async_copy(y_hbm.at[pl.ds(d, 1)], ybuf.at[slot, k, pl.ds(r, 1)], sem.at[slot]).start(priority=k)

    def wait(slot):
        for k in range(TOP_K):
            pltpu.make_async_copy(y_hbm.at[pl.ds(0, tm)], ybuf.at[slot, k], sem.at[slot]).wait()

    @pl.when(t == 0)
    def _():
        gather(0, 0)

    slot = lax.rem(t, 2)
    wait(slot)
    gather(jnp.minimum(t + 1, nsteps - 1), 1 - slot)
    w0 = wt_ref[:, 0:1]
    w1 = wt_ref[:, 1:2]
    half = h_ref.shape[1] // 2
    for c in range(0, half, LN_CHUNK):
        sl_a, sl_b = slice(c, c + LN_CHUNK), slice(half + c, half + c + LN_CHUNK)
        y0a, y0b = _unpack_bf16_pair(ybuf[slot, 0, :, sl_a])
        y1a, y1b = _unpack_bf16_pair(ybuf[slot, 1, :, sl_a])
        of_ref[:, sl_a] = ALPHA * h_ref[:, sl_a] + (y0a * w0 + y1a * w1)
        of_ref[:, sl_b] = ALPHA * h_ref[:, sl_b] + (y0b * w0 + y1b * w1)
    _layer_norm_inplace(of_ref, g_ref, b_ref, ob_ref)

    @pl.when(t == nsteps - 1)
    def _():
        wait(1 - slot)


def moe_combine_ln(y, h, wts, dest, g, b, layer, tm=256):
    N = wts.shape[0]
    D = g.shape[2]
    tm = min(tm, N)
    nsteps = N // tm
    vec = pl.BlockSpec((None, 1, D), lambda i, d: (layer, 0, 0))
    grid_spec = pltpu.PrefetchScalarGridSpec(
        num_scalar_prefetch=1,
        grid=(nsteps,),
        in_specs=[pl.BlockSpec(memory_space=pl.ANY),
                  pl.BlockSpec((tm, D), lambda i, d: (i, 0)),
                  pl.BlockSpec((tm, LANES), lambda i, d: (i, 0)),
                  vec, vec],
        out_specs=[pl.BlockSpec((tm, D), lambda i, d: (i, 0)),
                   pl.BlockSpec((tm, D), lambda i, d: (i, 0))],
        scratch_shapes=[pltpu.VMEM((2, TOP_K, tm, D // 2), jnp.uint32), pltpu.SemaphoreType.DMA((2,))],
    )
    return pl.pallas_call(
        functools.partial(_combine_ln_kernel, tm=tm, nsteps=nsteps),
        grid_spec=grid_spec,
        out_shape=[jax.ShapeDtypeStruct((N, D), F32), jax.ShapeDtypeStruct((N, D), BF16)],
        compiler_params=_params("arbitrary"),
        name="moe_combine_ln",
    )(dest, y, h, wts, g, b)


def _dispatch_plan(e_idx):
    N = e_idx.shape[0]
    A = N * TOP_K
    T = EXPERT_TILE
    n_tiles = (A + N_EXPERTS * (T - 1) + T - 1) // T
    flat_e = e_idx.reshape(-1)
    onehot = jax.nn.one_hot(flat_e, N_EXPERTS, dtype=jnp.int32)
    counts = jnp.sum(onehot, axis=0)
    rank = jnp.sum((jnp.cumsum(onehot, axis=0) - 1) * onehot, axis=1)
    padded = (counts + T - 1) // T * T
    pad_end = jnp.cumsum(padded)
    dest = (pad_end[flat_e] - padded[flat_e] + rank).astype(jnp.int32)
    slot_tok = jnp.zeros((n_tiles * T,), jnp.int32).at[dest].set(jnp.arange(A, dtype=jnp.int32) // TOP_K)
    starts = jnp.arange(n_tiles, dtype=jnp.int32) * T
    tile_e = jnp.minimum(jnp.sum((pad_end[None, :] <= starts[:, None]).astype(jnp.int32), axis=1), N_EXPERTS - 1)
    n_used = (pad_end[-1] // T).astype(jnp.int32).reshape(1)
    return tile_e.astype(jnp.int32), slot_tok, n_used, dest


def _rotary_table(positions):
    inv_freq = ROPE_THETA ** (-jnp.arange(0, ROT_DIM, 2, dtype=F32) / ROT_DIM)
    ang = positions.astype(F32).reshape(-1)[:, None] * inv_freq
    cos, sin = jnp.cos(ang), jnp.sin(ang)
    n = ang.shape[0]
    rest = HEAD_DIM - ROT_DIM
    z = lambda w: jnp.zeros((n, w), F32)
    c = jnp.concatenate([cos, cos, jnp.ones((n, rest), F32)], axis=1)
    s_lo = jnp.concatenate([-sin, z(HEAD_DIM - ROT_HALF)], axis=1)
    s_hi = jnp.concatenate([z(ROT_HALF), sin, z(rest)], axis=1)
    rep = LANES // HEAD_DIM
    return jnp.concatenate([jnp.tile(c, (1, rep)), jnp.tile(s_lo, (1, rep)), jnp.tile(s_hi, (1, rep))], axis=1)


def kernel(x, positions, even_w_in, even_w_out, attn_sinks, hgrn_lb_logits, hgrn_gnorm_w, rec_w_in, rec_conv_w, rec_conv_b, rec_gate_a_w, rec_gate_a_b, rec_gate_x_w, rec_gate_x_b, rec_lambda, rec_w_out, ln_mix_g, ln_mix_b, ln_ffn_g, ln_ffn_b, router_group_w, router_group_b, router_expert_w, router_expert_b, moe_w_gate, moe_w_up, moe_w_down):
    B, S, D = x.shape
    N = B * S
    tab = _rotary_table(positions)
    sm = jax.nn.softmax(hgrn_lb_logits.astype(F32), axis=0)
    lb_table = (jnp.cumsum(sm, axis=0) - sm[:1])[:, None, :]
    pad = LANES - N_GROUPS - N_EXPERTS
    router_w = jnp.concatenate([router_group_w, router_expert_w, jnp.zeros((DEPTH, D, pad), F32)], axis=-1)
    router_b = jnp.concatenate([router_group_b, router_expert_b, jnp.zeros((DEPTH, pad), F32)], axis=-1)[:, None, :]
    router_w_hi = router_w.astype(BF16)
    router_w_lo = (router_w - router_w_hi.astype(F32)).astype(BF16)
    router_w2 = jnp.concatenate([router_w_hi, router_w_lo], axis=-1)
    row = lambda a: a[:, None, :]
    even_w_out_b, rec_w_out_b = even_w_out.astype(BF16), rec_w_out.astype(BF16)
    wa_b, wx_b = rec_gate_a_w.astype(BF16), rec_gate_x_w.astype(BF16)
    gnorm = row(hgrn_gnorm_w)
    ln_mix_g3, ln_mix_b3, ln_ffn_g3, ln_ffn_b3 = row(ln_mix_g), row(ln_mix_b), row(ln_ffn_g), row(ln_ffn_b)

    h = x.reshape(N, D)
    hb = h.astype(BF16)
    for layer in range(DEPTH):
        j = layer // 2
        if layer % 2 == 0:
            proj = matmul(hb, even_w_in, j, F32)
            o_a = swa_attention(proj, tab, attn_sinks[j], B, S)
            o_b = hgrn2(proj, lb_table, gnorm, j, B, S)
            mixed = jnp.concatenate([o_a, *o_b], axis=1)
            w_out = even_w_out_b
        else:
            proj = matmul(hb, rec_w_in, j, F32)
            mixed = recurrent_block(proj, rec_conv_w, row(rec_conv_b), wa_b, row(rec_gate_a_b), wx_b,
                                    row(rec_gate_x_b), row(rec_lambda), j, B, S)
            w_out = rec_w_out_b
        h, hp, idx, wts = matmul_residual_ln_route(mixed, w_out, h, ln_mix_g3, ln_mix_b3, router_w2, router_b, layer)
        tile_e, slot_tok, n_used, dest = _dispatch_plan(idx[:, :TOP_K])
        y = moe_experts(hp, moe_w_gate, moe_w_up, moe_w_down, tile_e, slot_tok, n_used, layer)
        h, hb = moe_combine_ln(y, h, wts, dest, ln_ffn_g3, ln_ffn_b3, layer)
    return h.reshape(B, S, D)
```

```python
import functools

import jax
import jax.numpy as jnp
from jax import lax
from jax.experimental import pallas as pl
from jax.experimental.pallas import tpu as pltpu

F32 = jnp.float32
BF16 = jnp.bfloat16

D_MODEL = 4096
DEPTH = 4
ATTN_WIDTH = D_MODEL // 2
HEAD_DIM = 64
N_Q_HEADS = ATTN_WIDTH // HEAD_DIM
N_KV_HEADS = N_Q_HEADS // 8
Q_PER_KV = N_Q_HEADS // N_KV_HEADS
KV_WIDTH = N_KV_HEADS * HEAD_DIM
WINDOW = 128
ROT_DIM = HEAD_DIM // 4
ROT_HALF = ROT_DIM // 2
ROPE_THETA = 500000.0
MASK_VALUE = -1e9
HGRN_WIDTH = D_MODEL - ATTN_WIDTH
HGRN_DIM = 128
N_HGRN_HEADS = HGRN_WIDTH // HGRN_DIM
HGRN_CHUNK = 64
HGRN_SUB = 16
HGRN_HEADS_PER_BLOCK = 4
HGRN_BLOCKS_PER_STEP = 2
HGRN_PIVOT_MAX_DECAY = 75.0
LRU_WIDTH = D_MODEL
N_LRU_HEADS = 16
LRU_HEAD_DIM = LRU_WIDTH // N_LRU_HEADS
CONV_WIDTH = 4
RG_C = 8.0
LRU_SCAN_BLOCK = 32
N_GROUPS = 4
EXPERTS_PER_GROUP = 8
N_EXPERTS = N_GROUPS * EXPERTS_PER_GROUP
TOP_K = 2
EXPERT_FF = 3 * D_MODEL // 32
EXPERT_TILE = 256
ALPHA = (2.0 * DEPTH) ** 0.25
LN_EPS = 1e-5
RMS_EPS = 1e-6

LANES = 128
SUBLANES = 8
VMEM_LIMIT = 56 * 1024 * 1024
NEG_BIG = -1e30

_NT = (((1,), (1,)), ((), ()))
_TN = (((0,), (0,)), ((), ()))


def _params(*sem):
    return pltpu.CompilerParams(dimension_semantics=sem, vmem_limit_bytes=VMEM_LIMIT)


def _mm_kernel(x_ref, w_ref, o_ref):
    o_ref[...] = jnp.dot(x_ref[...], w_ref[...].astype(BF16), preferred_element_type=F32).astype(o_ref.dtype)


def matmul(x, w, layer, out_dtype, tm=1024, tn=512):
    M, K = x.shape
    N = w.shape[2]
    tm = min(tm, M)
    return pl.pallas_call(
        _mm_kernel,
        grid=(M // tm, N // tn),
        in_specs=[pl.BlockSpec((tm, K), lambda i, j: (i, 0)),
                  pl.BlockSpec((None, K, tn), lambda i, j: (layer, 0, j))],
        out_specs=pl.BlockSpec((tm, tn), lambda i, j: (i, j)),
        out_shape=jax.ShapeDtypeStruct((M, N), out_dtype),
        compiler_params=_params("parallel", "arbitrary"),
        name="matmul",
    )(x, w)


LN_CHUNK = 512
HI_MASK = 0xFFFF0000


def _pack_bf16_pair(a, b):
    ua = lax.bitcast_convert_type(a.astype(BF16).astype(F32), jnp.uint32)
    ub = lax.bitcast_convert_type(b.astype(BF16).astype(F32), jnp.uint32)
    return (ua & jnp.uint32(HI_MASK)) | (ub >> 16)


def _unpack_bf16_pair(w):
    a = lax.bitcast_convert_type(w & jnp.uint32(HI_MASK), F32)
    b = lax.bitcast_convert_type(w << 16, F32)
    return a, b


def _layer_norm_inplace(of_ref, g_ref, b_ref, ob_ref=None, op_ref=None):
    D = of_ref.shape[1]
    assert D % (2 * LN_CHUNK) == 0
    cols = [slice(c, c + LN_CHUNK) for c in range(0, D, LN_CHUNK)]
    acc = None
    for sl in cols:
        acc = of_ref[:, sl] if acc is None else acc + of_ref[:, sl]
    mu = jnp.sum(acc, axis=-1, keepdims=True) * (1.0 / D)
    acc = None
    for sl in cols:
        d = of_ref[:, sl] - mu
        acc = d * d if acc is None else acc + d * d
    rstd = lax.rsqrt(jnp.sum(acc, axis=-1, keepdims=True) * (1.0 / D) + LN_EPS)
    def normalize(sl):
        y = (of_ref[:, sl] - mu) * rstd * g_ref[:, sl] + b_ref[:, sl]
        of_ref[:, sl] = y
        if ob_ref is not None:
            ob_ref[:, sl] = y.astype(BF16)
        return y

    for c in range(0, D // 2, LN_CHUNK):
        ya = normalize(slice(c, c + LN_CHUNK))
        yb = normalize(slice(D // 2 + c, D // 2 + c + LN_CHUNK))
        if op_ref is not None:
            op_ref[:, c:c + LN_CHUNK] = _pack_bf16_pair(ya, yb)


ROUTE_ROWS = 256


def _route(h, w_ref, b_ref):
    n = h.shape[0]
    h_hi = h.astype(BF16)
    h_lo = (h - h_hi.astype(F32)).astype(BF16)
    parts = jnp.dot(jnp.concatenate([h_hi, h_lo], axis=0), w_ref[...], preferred_element_type=F32)
    logits = (parts[:n, :LANES] + (parts[:n, LANES:] + parts[n:, :LANES]) + parts[n:, LANES:]) + b_ref[...]
    lane = lax.broadcasted_iota(jnp.int32, logits.shape, 1)
    neg = -jnp.inf
    big = jnp.int32(LANES)
    gl = jnp.where(lane < N_GROUPS, logits, neg)
    gmax = jnp.max(gl, axis=-1, keepdims=True)
    g_sel = jnp.min(jnp.where(gl == gmax, lane, big), axis=-1, keepdims=True)
    p_top = 1.0 / jnp.sum(jnp.exp(gl - gmax), axis=-1, keepdims=True)
    lo = N_GROUPS + g_sel * EXPERTS_PER_GROUP
    el = jnp.where((lane >= lo) & (lane < lo + EXPERTS_PER_GROUP), logits, neg)
    m1 = jnp.max(el, axis=-1, keepdims=True)
    i1 = jnp.min(jnp.where(el == m1, lane, big), axis=-1, keepdims=True)
    el2 = jnp.where(lane == i1, neg, el)
    m2 = jnp.max(el2, axis=-1, keepdims=True)
    i2 = jnp.min(jnp.where(el2 == m2, lane, big), axis=-1, keepdims=True)
    z = jnp.sum(jnp.exp(el - m1), axis=-1, keepdims=True)
    p1 = 1.0 / z
    p2 = jnp.exp(m2 - m1) / z
    tot = p1 + p2
    last = N_EXPERTS - 1
    idx = jnp.where(lane == 0, jnp.minimum(i1 - N_GROUPS, last), jnp.where(lane == 1, jnp.minimum(i2 - N_GROUPS, last), 0))
    wt = jnp.where(lane == 0, p1 / tot * p_top, jnp.where(lane == 1, p2 / tot * p_top, 0.0))
    return idx, wt


def _mm_ln_route_kernel(x_ref, w_ref, res_ref, g_ref, b_ref, rw_ref, rb_ref, of_ref, op_ref, idx_ref, wt_ref,
                        *, nj, tn):
    j = pl.program_id(1)
    pre = ALPHA * res_ref[...] + jnp.dot(x_ref[...], w_ref[...], preferred_element_type=F32)
    for c in range(nj):
        @pl.when(j == c)
        def _(c=c):
            of_ref[:, c * tn:(c + 1) * tn] = pre

    @pl.when(j == nj - 1)
    def _():
        _layer_norm_inplace(of_ref, g_ref, b_ref, op_ref=op_ref)

        def route_rows(r, carry):
            rows = pl.ds(pl.multiple_of(r * ROUTE_ROWS, ROUTE_ROWS), ROUTE_ROWS)
            idx_ref[rows, :], wt_ref[rows, :] = _route(of_ref[rows, :], rw_ref, rb_ref)
            return carry

        lax.fori_loop(0, of_ref.shape[0] // ROUTE_ROWS, route_rows, 0)


def matmul_residual_ln_route(x, w, res, g, b, rw, rb, layer, tm=512, tn=512):
    M, K = x.shape
    N = w.shape[2]
    tm = min(tm, M)
    nj = N // tn
    vec = pl.BlockSpec((None, 1, N), lambda i, j: (layer, 0, 0))
    return pl.pallas_call(
        functools.partial(_mm_ln_route_kernel, nj=nj, tn=tn),
        grid=(M // tm, nj),
        in_specs=[pl.BlockSpec((tm, K), lambda i, j: (i, 0)),
                  pl.BlockSpec((None, K, tn), lambda i, j: (layer // 2, 0, j)),
                  pl.BlockSpec((tm, tn), lambda i, j: (i, j)),
                  vec, vec,
                  pl.BlockSpec((None, N, 2 * LANES), lambda i, j: (layer, 0, 0)),
                  pl.BlockSpec((None, 1, LANES), lambda i, j: (layer, 0, 0))],
        out_specs=[pl.BlockSpec((tm, N), lambda i, j: (i, 0)),
                   pl.BlockSpec((tm, N // 2), lambda i, j: (i, 0)),
                   pl.BlockSpec((tm, LANES), lambda i, j: (i, 0)),
                   pl.BlockSpec((tm, LANES), lambda i, j: (i, 0))],
        out_shape=[jax.ShapeDtypeStruct((M, N), F32),
                   jax.ShapeDtypeStruct((M, N // 2), jnp.uint32),
                   jax.ShapeDtypeStruct((M, LANES), jnp.int32),
                   jax.ShapeDtypeStruct((M, LANES), F32)],
        compiler_params=_params("parallel", "arbitrary"),
        name="matmul_residual_ln_route",
    )(x, w, res, g, b, rw, rb)


def _rotary(x, tab):
    c, s_lo, s_hi = tab[:, :LANES], tab[:, LANES:2 * LANES], tab[:, 2 * LANES:]
    out = []
    for i in range(x.shape[1] // LANES):
        xc = x[:, i * LANES:(i + 1) * LANES]
        out.append(xc * c + pltpu.roll(xc, LANES - ROT_HALF, 1) * s_lo + pltpu.roll(xc, ROT_HALF, 1) * s_hi)
    return out


def _attn_kernel(sink_ref, q_ref, kc_ref, vc_ref, kp_ref, vp_ref, tc_ref, tp_ref, mask_ref, o_ref):
    W = WINDOW
    tab_c = tc_ref[...]
    tab_p = tp_ref[...]
    q_cols = _rotary(q_ref[...], tab_c)
    kc = jnp.concatenate(_rotary(kc_ref[...], tab_c), axis=1)
    kp = jnp.concatenate(_rotary(kp_ref[...], tab_p), axis=1)
    k_all = jnp.concatenate([kp, kc], axis=0).astype(BF16)
    v_all = jnp.concatenate([vp_ref[...], vc_ref[...]], axis=0).astype(BF16)
    outs = []
    for kv in range(N_KV_HEADS):
        heads = range(kv * Q_PER_KV, (kv + 1) * Q_PER_KV)
        qg = jnp.concatenate(
            [q_cols[h // 2][:, (h % 2) * HEAD_DIM:(h % 2 + 1) * HEAD_DIM] for h in heads], axis=0).astype(BF16)
        kh = k_all[:, kv * HEAD_DIM:(kv + 1) * HEAD_DIM]
        vh = v_all[:, kv * HEAD_DIM:(kv + 1) * HEAD_DIM]
        s_all = lax.dot_general(qg, kh, _NT, preferred_element_type=F32)
        ps, denoms = [], []
        for g, h in enumerate(heads):
            s = s_all[g * W:(g + 1) * W] * mask_ref[:, :2 * W] + mask_ref[:, 2 * W:4 * W]
            sink = sink_ref[h]
            m = jnp.maximum(jnp.max(s, axis=-1, keepdims=True), sink)
            p = jnp.exp(s - m) * mask_ref[:, 4 * W:]
            denoms.append(jnp.sum(p, axis=-1, keepdims=True) + jnp.exp(sink - m))
            ps.append(p.astype(BF16))
        o_all = jnp.dot(jnp.concatenate(ps, axis=0), vh, preferred_element_type=F32)
        for g in range(Q_PER_KV):
            outs.append(o_all[g * W:(g + 1) * W] / denoms[g])
    o_ref[...] = jnp.concatenate(outs, axis=1).astype(o_ref.dtype)


def _band_masks():
    qi = jnp.arange(WINDOW)[:, None]
    kj = jnp.arange(2 * WINDOW)[None, :]
    rel = qi + WINDOW - kj
    in_band = (rel >= 0) & (rel < WINDOW)
    out = []
    for has_prev in (False, True):
        keep = (in_band & ((kj >= WINDOW) | has_prev)).astype(F32)
        out.append(jnp.concatenate([keep * (HEAD_DIM ** -0.5), (1.0 - keep) * MASK_VALUE, keep], axis=1))
    return jnp.stack(out)


def swa_attention(proj, tab, sinks, batch, seq):
    nb = seq // WINDOW
    kcol = ATTN_WIDTH // KV_WIDTH
    cur = lambda b, n: b * nb + n
    prev = lambda b, n: b * nb + jnp.maximum(n - 1, 0)
    return pl.pallas_call(
        _attn_kernel,
        grid=(batch, nb),
        in_specs=[pl.BlockSpec(memory_space=pltpu.SMEM),
                  pl.BlockSpec((WINDOW, ATTN_WIDTH), lambda b, n: (cur(b, n), 0)),
                  pl.BlockSpec((WINDOW, KV_WIDTH), lambda b, n: (cur(b, n), kcol)),
                  pl.BlockSpec((WINDOW, KV_WIDTH), lambda b, n: (cur(b, n), kcol + 1)),
                  pl.BlockSpec((WINDOW, KV_WIDTH), lambda b, n: (prev(b, n), kcol)),
                  pl.BlockSpec((WINDOW, KV_WIDTH), lambda b, n: (prev(b, n), kcol + 1)),
                  pl.BlockSpec((WINDOW, 3 * LANES), lambda b, n: (cur(b, n), 0)),
                  pl.BlockSpec((WINDOW, 3 * LANES), lambda b, n: (prev(b, n), 0)),
                  pl.BlockSpec((None, WINDOW, 6 * WINDOW), lambda b, n: (jnp.minimum(n, 1), 0, 0))],
        out_specs=pl.BlockSpec((WINDOW, ATTN_WIDTH), lambda b, n: (cur(b, n), 0)),
        out_shape=jax.ShapeDtypeStruct((batch * seq, ATTN_WIDTH), BF16),
        compiler_params=_params("parallel", "parallel"),
        name="swa_attention",
    )(sinks, proj, proj, proj, proj, proj, tab, tab, _band_masks())


def _hgrn_gates(q_in, z, lb):
    C = HGRN_CHUNK
    sg = jax.nn.sigmoid(z)
    lf = jnp.log(lb + (1.0 - lb) * sg)
    k = (1.0 - lb) * jax.nn.sigmoid(-z)
    q = q_in * jax.nn.sigmoid(q_in)
    ti = lax.broadcasted_iota(jnp.int32, (C, C), 0)
    si = lax.broadcasted_iota(jnp.int32, (C, C), 1)
    tril = jnp.where(ti >= si, 1.0, 0.0).astype(F32)
    b = jnp.dot(tril, lf, preferred_element_type=F32, precision=lax.Precision.HIGHEST)
    return q, k, b


def _hgrn_intra_pivot(q, k, v, b):
    C = HGRN_CHUNK
    sc = lax.dot_general((q * jnp.exp(b)).astype(BF16), (k * jnp.exp(-b)).astype(BF16), _NT,
                         preferred_element_type=F32)
    ti = lax.broadcasted_iota(jnp.int32, (C, C), 0)
    si = lax.broadcasted_iota(jnp.int32, (C, C), 1)
    sc = jnp.where(ti >= si, sc, 0.0)
    return jnp.dot(sc.astype(BF16), v.astype(BF16), preferred_element_type=F32)


def _hgrn_intra_safe(q, k, v, b):
    C, c = HGRN_CHUNK, HGRN_SUB
    half = c // 2
    rows = lax.broadcasted_iota(jnp.int32, (half, HGRN_DIM), 0)
    pieces = [None] * (C // c)
    for j in range(C // c):
        lo, hi = j * c, (j + 1) * c
        bd, qd, kd, vd = b[lo:hi], q[lo:hi], k[lo:hi], v[lo:hi]
        e_j = b[hi - 1:hi, :]
        if hi < C:
            khat = (kd * jnp.exp(e_j - bd)).astype(BF16)
            qt = (q[hi:] * jnp.exp(b[hi:] - e_j)).astype(BF16)
            sc = lax.dot_general(qt, khat, _NT, preferred_element_type=F32)
            upd = jnp.dot(sc.astype(BF16), vd.astype(BF16), preferred_element_type=F32)
            for jj in range(j + 1, C // c):
                u = upd[(jj - j - 1) * c:(jj - j) * c]
                pieces[jj] = u if pieces[jj] is None else pieces[jj] + u
        acc = [jnp.zeros((half, HGRN_DIM), F32), jnp.zeros((half, HGRN_DIM), F32)]
        for s in range(c):
            for hf in range(s // half, 2):
                r0 = hf * half
                diff = bd[r0:r0 + half] - bd[s:s + 1]
                if s >= r0:
                    diff = jnp.where(rows >= s - r0, diff, NEG_BIG)
                col = jnp.sum(qd[r0:r0 + half] * kd[s:s + 1] * jnp.exp(diff), axis=-1, keepdims=True)
                acc[hf] = acc[hf] + col * vd[s:s + 1]
        d = jnp.concatenate(acc, axis=0)
        pieces[j] = d if pieces[j] is None else pieces[j] + d
    return jnp.concatenate(pieces, axis=0)


def _hgrn_finish(o_intra, q, k, v, b, g_in, gn, state_ref):
    C = HGRN_CHUNK
    b_last = b[C - 1:C, :]
    state = state_ref[...]
    o = o_intra + lax.dot_general((q * jnp.exp(b)).astype(BF16), state.astype(BF16), _NT,
                                  preferred_element_type=F32)
    kdec = (k * jnp.exp(b_last - b)).astype(BF16)
    state_ref[...] = state * jnp.exp(b_last) + lax.dot_general(v.astype(BF16), kdec, _TN, preferred_element_type=F32)
    ms = jnp.mean(o * o, axis=-1, keepdims=True)
    return o * lax.rsqrt(ms + RMS_EPS) * gn * (g_in * jax.nn.sigmoid(g_in))


def _hgrn_kernel(*refs, n_chunks):
    nb, hb = HGRN_BLOCKS_PER_STEP, HGRN_HEADS_PER_BLOCK
    q_refs, f_refs, i_refs, g_refs, lb_refs = (refs[k * nb:(k + 1) * nb] for k in range(5))
    gn_ref = refs[5 * nb]
    o_refs = refs[5 * nb + 1:6 * nb + 1]
    state_ref = refs[6 * nb + 1]

    @pl.when(pl.program_id(2) == 0)
    def _():
        state_ref[...] = jnp.zeros_like(state_ref)

    gn = gn_ref[...]
    heads = [(blk, slice(hd * HGRN_DIM, (hd + 1) * HGRN_DIM)) for blk in range(nb) for hd in range(hb)]

    def body(ci, carry):
        r0 = pl.multiple_of(ci * HGRN_CHUNK, HGRN_CHUNK)
        rs = pl.ds(r0, HGRN_CHUNK)
        gates = [_hgrn_gates(q_refs[blk][rs, cs], f_refs[blk][rs, cs], lb_refs[blk][:, cs]) for blk, cs in heads]
        total = gates[0][2][HGRN_CHUNK - 1:HGRN_CHUNK, :]
        for _, _, b in gates[1:]:
            total = jnp.minimum(total, b[HGRN_CHUNK - 1:HGRN_CHUNK, :])
        mild = jnp.min(total) >= -HGRN_PIVOT_MAX_DECAY

        def run(intra):
            for n, (blk, cs) in enumerate(heads):
                q, k, b = gates[n]
                v = i_refs[blk][rs, cs]
                y = _hgrn_finish(intra(q, k, v, b), q, k, v, b, g_refs[blk][rs, cs], gn, state_ref.at[n])
                o_refs[blk][rs, cs] = y.astype(o_refs[blk].dtype)

        pl.when(mild)(lambda: run(_hgrn_intra_pivot))
        pl.when(jnp.logical_not(mild))(lambda: run(_hgrn_intra_safe))
        return carry

    lax.fori_loop(0, n_chunks, body, 0)


def hgrn2(proj, lb, gnorm, layer, batch, seq, tile=512):
    tile = min(tile, seq)
    nt = seq // tile
    nb = HGRN_BLOCKS_PER_STEP
    w = HGRN_HEADS_PER_BLOCK * HGRN_DIM
    c0 = (ATTN_WIDTH + 2 * KV_WIDTH) // w
    per_kind = HGRN_WIDTH // w
    per_slab = per_kind // nb
    spec = lambda k, blk: pl.BlockSpec(
        (tile, w), lambda b, h, t: (b * nt + t, c0 + k * per_kind + blk * per_slab + h))
    blocks = range(nb)
    return pl.pallas_call(
        functools.partial(_hgrn_kernel, n_chunks=tile // HGRN_CHUNK),
        grid=(batch, per_slab, nt),
        in_specs=([spec(k, blk) for k in range(4) for blk in blocks]
                  + [pl.BlockSpec((None, 1, w), lambda b, h, t, blk=blk: (layer, 0, blk * per_slab + h))
                     for blk in blocks]
                  + [pl.BlockSpec((None, 1, HGRN_DIM), lambda b, h, t: (layer, 0, 0))]),
        out_specs=[pl.BlockSpec((tile, w), lambda b, h, t: (b * nt + t, h)) for _ in blocks],
        out_shape=[jax.ShapeDtypeStruct((batch * seq, HGRN_WIDTH // nb), BF16) for _ in blocks],
        scratch_shapes=[pltpu.VMEM((nb * HGRN_HEADS_PER_BLOCK, HGRN_DIM, HGRN_DIM), F32)],
        compiler_params=_params("parallel", "parallel", "arbitrary"),
        name="hgrn2",
    )(*([proj] * (4 * nb)), *([lb] * nb), gnorm)


def _lru_kernel(gate_ref, rnn_ref, cw_ref, cb_ref, wa_ref, wx_ref, ba_ref, bx_ref, lam_ref, o_ref,
                xs_ref, h_ref, *, tile):
    @pl.when(pl.program_id(2) == 0)
    def _():
        xs_ref[0:SUBLANES, :] = jnp.zeros((SUBLANES, LRU_HEAD_DIM), F32)
        h_ref[...] = jnp.zeros_like(h_ref)

    x = rnn_ref[...]
    xs_ref[SUBLANES:, :] = x
    u = cb_ref[...] + x * cw_ref[CONV_WIDTH - 1:CONV_WIDTH, :]
    for j in range(CONV_WIDTH - 1):
        back = CONV_WIDTH - 1 - j
        u = u + xs_ref[pl.ds(SUBLANES - back, tile), :] * cw_ref[j:j + 1, :]
    xs_ref[0:SUBLANES, :] = x[tile - SUBLANES:, :]

    ub = u.astype(BF16)
    r = jax.nn.sigmoid(jnp.dot(ub, wa_ref[...], preferred_element_type=F32) + ba_ref[...])
    ig = jax.nn.sigmoid(jnp.dot(ub, wx_ref[...], preferred_element_type=F32) + bx_ref[...])
    nl = -lam_ref[...]
    softplus = jnp.maximum(nl, 0.0) + jnp.log1p(jnp.exp(-jnp.abs(nl)))
    log_a = -RG_C * r * softplus
    a = jnp.exp(log_a)
    bb = jnp.sqrt(jnp.maximum(-jnp.tanh(log_a) * (a * a + 1.0), 0.0)) * (ig * u)

    blk = min(LRU_SCAN_BLOCK, tile)
    rows = lax.broadcasted_iota(jnp.int32, (tile, LRU_HEAD_DIM), 0) & (blk - 1)
    k = 1
    while k < blk:
        keep = rows >= k
        a_sh = jnp.where(keep, pltpu.roll(a, k, 0), 1.0)
        b_sh = jnp.where(keep, pltpu.roll(bb, k, 0), 0.0)
        bb = a * b_sh + bb
        a = a * a_sh
        k *= 2
    carry = h_ref[...]
    pieces = []
    for r in range(0, tile, blk):
        hb = a[r:r + blk] * carry + bb[r:r + blk]
        pieces.append(hb)
        carry = hb[blk - 1:blk, :]
    h = jnp.concatenate(pieces, axis=0)
    h_ref[...] = carry
    o_ref[...] = (h * jax.nn.gelu(gate_ref[...])).astype(o_ref.dtype)


def recurrent_block(proj, conv_w, conv_b, wa, ba, wx, bx, lam, layer, batch, seq, tile=256):
    tile = min(tile, seq)
    nt = seq // tile
    hd = LRU_HEAD_DIM
    vec = pl.BlockSpec((None, 1, hd), lambda b, h, t: (layer, 0, h))
    mat = pl.BlockSpec((None, None, hd, hd), lambda b, h, t: (layer, h, 0, 0))
    return pl.pallas_call(
        functools.partial(_lru_kernel, tile=tile),
        grid=(batch, N_LRU_HEADS, nt),
        in_specs=[pl.BlockSpec((tile, hd), lambda b, h, t: (b * nt + t, h)),
                  pl.BlockSpec((tile, hd), lambda b, h, t: (b * nt + t, N_LRU_HEADS + h)),
                  pl.BlockSpec((None, CONV_WIDTH, hd), lambda b, h, t: (layer, 0, h)),
                  vec, mat, mat, vec, vec, vec],
        out_specs=pl.BlockSpec((tile, hd), lambda b, h, t: (b * nt + t, h)),
        out_shape=jax.ShapeDtypeStruct((batch * seq, LRU_WIDTH), BF16),
        scratch_shapes=[pltpu.VMEM((tile + SUBLANES, hd), F32), pltpu.VMEM((1, hd), F32)],
        compiler_params=_params("parallel", "parallel", "arbitrary"),
        name="recurrent_block",
    )(proj, proj, conv_w, conv_b, wa, wx, ba, bx, lam)


def _expert_kernel(te_ref, tok_ref, nused_ref, h_hbm, wg_ref, wu_ref, wd_ref, y_ref, xbuf, sem):
    t = pl.program_id(0)
    nused = nused_ref[0]
    T = EXPERT_TILE

    def gather(tile, slot):
        base = tile * T
        for r in range(T):
            tok = tok_ref[base + r]
            pltpu.make_async_copy(h_hbm.at[pl.ds(tok, 1)], xbuf.at[slot, pl.ds(r, 1)], sem.at[slot]).start()

    def wait(slot):
        pltpu.make_async_copy(h_hbm.at[pl.ds(0, T)], xbuf.at[slot], sem.at[slot]).wait()

    @pl.when(t == 0)
    def _():
        gather(0, 0)

    slot = lax.rem(t, 2)

    @pl.when(t < nused)
    def _():
        wait(slot)
        gather(jnp.minimum(t + 1, nused - 1), 1 - slot)
        xa, xb = _unpack_bf16_pair(xbuf[slot])
        x = jnp.concatenate([xa.astype(BF16), xb.astype(BF16)], axis=1)
        gate = jnp.dot(x, wg_ref[...].astype(BF16), preferred_element_type=F32)
        up = jnp.dot(x, wu_ref[...].astype(BF16), preferred_element_type=F32)
        hid = (gate * jax.nn.sigmoid(gate) * up).astype(BF16)
        y = jnp.dot(hid, wd_ref[...].astype(BF16), preferred_element_type=F32)
        half = y.shape[1] // 2
        y_ref[...] = _pack_bf16_pair(y[:, :half], y[:, half:])

        @pl.when(t == nused - 1)
        def _():
            wait(1 - slot)

    @pl.when(t >= nused)
    def _():
        y_ref[...] = jnp.zeros_like(y_ref)


def moe_experts(h, wg, wu, wd, tile_e, slot_tok, n_used, layer):
    n_tiles = tile_e.shape[0]
    T = EXPERT_TILE
    D = wg.shape[2]
    grid_spec = pltpu.PrefetchScalarGridSpec(
        num_scalar_prefetch=3,
        grid=(n_tiles,),
        in_specs=[pl.BlockSpec(memory_space=pl.ANY),
                  pl.BlockSpec((None, None, D, EXPERT_FF), lambda i, te, tok, nu: (layer, te[i], 0, 0)),
                  pl.BlockSpec((None, None, D, EXPERT_FF), lambda i, te, tok, nu: (layer, te[i], 0, 0)),
                  pl.BlockSpec((None, None, EXPERT_FF, D), lambda i, te, tok, nu: (layer, te[i], 0, 0))],
        out_specs=pl.BlockSpec((T, D // 2), lambda i, te, tok, nu: (i, 0)),
        scratch_shapes=[pltpu.VMEM((2, T, D // 2), jnp.uint32), pltpu.SemaphoreType.DMA((2,))],
    )
    return pl.pallas_call(
        _expert_kernel,
        grid_spec=grid_spec,
        out_shape=jax.ShapeDtypeStruct((n_tiles * T, D // 2), jnp.uint32),
        compiler_params=_params("arbitrary"),
        name="moe_experts",
    )(tile_e, slot_tok, n_used, h, wg, wu, wd)


def _combine_ln_kernel(dest_ref, y_hbm, h_ref, wt_ref, g_ref, b_ref, of_ref, ob_ref, ybuf, sem, *, tm, nsteps):
    t = pl.program_id(0)

    def gather(step, slot):
        base = step * (tm * TOP_K)
        for r in range(tm):
            for k in range(TOP_K):
                d = dest_ref[base + r * TOP_K + k]
                pltpu.make_async_copy(y_hbm.at[pl.ds(d, 1)], ybuf.at[slot, k, pl.ds(r, 1)], sem.at[slot]).start(priority=k)

    def wait(slot):
        for k in range(TOP_K):
            pltpu.make_async_copy(y_hbm.at[pl.ds(0, tm)], ybuf.at[slot, k], sem.at[slot]).wait()

    @pl.when(t == 0)
    def _():
        gather(0, 0)

    slot = lax.rem(t, 2)
    wait(slot)
    gather(jnp.minimum(t + 1, nsteps - 1), 1 - slot)
    w0 = wt_ref[:, 0:1]
    w1 = wt_ref[:, 1:2]
    half = h_ref.shape[1] // 2
    for c in range(0, half, LN_CHUNK):
        sl_a, sl_b = slice(c, c + LN_CHUNK), slice(half + c, half + c + LN_CHUNK)
        y0a, y0b = _unpack_bf16_pair(ybuf[slot, 0, :, sl_a])
        y1a, y1b = _unpack_bf16_pair(ybuf[slot, 1, :, sl_a])
        of_ref[:, sl_a] = ALPHA * h_ref[:, sl_a] + (y0a * w0 + y1a * w1)
        of_ref[:, sl_b] = ALPHA * h_ref[:, sl_b] + (y0b * w0 + y1b * w1)
    _layer_norm_inplace(of_ref, g_ref, b_ref, ob_ref)

    @pl.when(t == nsteps - 1)
    def _():
        wait(1 - slot)


def moe_combine_ln(y, h, wts, dest, g, b, layer, tm=256):
    N = wts.shape[0]
    D = g.shape[2]
    tm = min(tm, N)
    nsteps = N // tm
    vec = pl.BlockSpec((None, 1, D), lambda i, d: (layer, 0, 0))
    grid_spec = pltpu.PrefetchScalarGridSpec(
        num_scalar_prefetch=1,
        grid=(nsteps,),
        in_specs=[pl.BlockSpec(memory_space=pl.ANY),
                  pl.BlockSpec((tm, D), lambda i, d: (i, 0)),
                  pl.BlockSpec((tm, LANES), lambda i, d: (i, 0)),
                  vec, vec],
        out_specs=[pl.BlockSpec((tm, D), lambda i, d: (i, 0)),
                   pl.BlockSpec((tm, D), lambda i, d: (i, 0))],
        scratch_shapes=[pltpu.VMEM((2, TOP_K, tm, D // 2), jnp.uint32), pltpu.SemaphoreType.DMA((2,))],
    )
    return pl.pallas_call(
        functools.partial(_combine_ln_kernel, tm=tm, nsteps=nsteps),
        grid_spec=grid_spec,
        out_shape=[jax.ShapeDtypeStruct((N, D), F32), jax.ShapeDtypeStruct((N, D), BF16)],
        compiler_params=_params("arbitrary"),
        name="moe_combine_ln",
    )(dest, y, h, wts, g, b)


def _dispatch_plan(e_idx):
    N = e_idx.shape[0]
    A = N * TOP_K
    T = EXPERT_TILE
    n_tiles = (A + N_EXPERTS * (T - 1) + T - 1) // T
    flat_e = e_idx.reshape(-1)
    onehot = jax.nn.one_hot(flat_e, N_EXPERTS, dtype=jnp.int32)
    counts = jnp.sum(onehot, axis=0)
    rank = jnp.sum((jnp.cumsum(onehot, axis=0) - 1) * onehot, axis=1)
    padded = (counts + T - 1) // T * T
    pad_end = jnp.cumsum(padded)
    dest = (pad_end[flat_e] - padded[flat_e] + rank).astype(jnp.int32)
    slot_tok = jnp.zeros((n_tiles * T,), jnp.int32).at[dest].set(jnp.arange(A, dtype=jnp.int32) // TOP_K)
    starts = jnp.arange(n_tiles, dtype=jnp.int32) * T
    tile_e = jnp.minimum(jnp.sum((pad_end[None, :] <= starts[:, None]).astype(jnp.int32), axis=1), N_EXPERTS - 1)
    n_used = (pad_end[-1] // T).astype(jnp.int32).reshape(1)
    return tile_e.astype(jnp.int32), slot_tok, n_used, dest


def _rotary_table(positions):
    inv_freq = ROPE_THETA ** (-jnp.arange(0, ROT_DIM, 2, dtype=F32) / ROT_DIM)
    ang = positions.astype(F32).reshape(-1)[:, None] * inv_freq
    cos, sin = jnp.cos(ang), jnp.sin(ang)
    n = ang.shape[0]
    rest = HEAD_DIM - ROT_DIM
    z = lambda w: jnp.zeros((n, w), F32)
    c = jnp.concatenate([cos, cos, jnp.ones((n, rest), F32)], axis=1)
    s_lo = jnp.concatenate([-sin, z(HEAD_DIM - ROT_HALF)], axis=1)
    s_hi = jnp.concatenate([z(ROT_HALF), sin, z(rest)], axis=1)
    rep = LANES // HEAD_DIM
    return jnp.concatenate([jnp.tile(c, (1, rep)), jnp.tile(s_lo, (1, rep)), jnp.tile(s_hi, (1, rep))], axis=1)


def kernel(x, positions, even_w_in, even_w_out, attn_sinks, hgrn_lb_logits, hgrn_gnorm_w, rec_w_in, rec_conv_w, rec_conv_b, rec_gate_a_w, rec_gate_a_b, rec_gate_x_w, rec_gate_x_b, rec_lambda, rec_w_out, ln_mix_g, ln_mix_b, ln_ffn_g, ln_ffn_b, router_group_w, router_group_b, router_expert_w, router_expert_b, moe_w_gate, moe_w_up, moe_w_down):
    B, S, D = x.shape
    N = B * S
    tab = _rotary_table(positions)
    sm = jax.nn.softmax(hgrn_lb_logits.astype(F32), axis=0)
    lb_table = (jnp.cumsum(sm, axis=0) - sm[:1])[:, None, :]
    pad = LANES - N_GROUPS - N_EXPERTS
    router_w = jnp.concatenate([router_group_w, router_expert_w, jnp.zeros((DEPTH, D, pad), F32)], axis=-1)
    router_b = jnp.concatenate([router_group_b, router_expert_b, jnp.zeros((DEPTH, pad), F32)], axis=-1)[:, None, :]
    router_w_hi = router_w.astype(BF16)
    router_w_lo = (router_w - router_w_hi.astype(F32)).astype(BF16)
    router_w2 = jnp.concatenate([router_w_hi, router_w_lo], axis=-1)
    row = lambda a: a[:, None, :]
    even_w_out_b, rec_w_out_b = even_w_out.astype(BF16), rec_w_out.astype(BF16)
    wa_b, wx_b = rec_gate_a_w.astype(BF16), rec_gate_x_w.astype(BF16)
    gnorm = row(hgrn_gnorm_w)
    ln_mix_g3, ln_mix_b3, ln_ffn_g3, ln_ffn_b3 = row(ln_mix_g), row(ln_mix_b), row(ln_ffn_g), row(ln_ffn_b)

    h = x.reshape(N, D)
    hb = h.astype(BF16)
    for layer in range(DEPTH):
        j = layer // 2
        if layer % 2 == 0:
            proj = matmul(hb, even_w_in, j, F32)
            o_a = swa_attention(proj, tab, attn_sinks[j], B, S)
            o_b = hgrn2(proj, lb_table, gnorm, j, B, S)
            mixed = jnp.concatenate([o_a, *o_b], axis=1)
            w_out = even_w_out_b
        else:
            proj = matmul(hb, rec_w_in, j, F32)
            mixed = recurrent_block(proj, rec_conv_w, row(rec_conv_b), wa_b, row(rec_gate_a_b), wx_b,
                                    row(rec_gate_x_b), row(rec_lambda), j, B, S)
            w_out = rec_w_out_b
        h, hp, idx, wts = matmul_residual_ln_route(mixed, w_out, h, ln_mix_g3, ln_mix_b3, router_w2, router_b, layer)
        tile_e, slot_tok, n_used, dest = _dispatch_plan(idx[:, :TOP_K])
        y = moe_experts(hp, moe_w_gate, moe_w_up, moe_w_down, tile_e, slot_tok, n_used, layer)
        h, hb = moe_combine_ln(y, h, wts, dest, ln_ffn_g3, ln_ffn_b3, layer)
    return h.reshape(B, S, D)
```

```python
import functools

import jax
import jax.numpy as jnp
from jax import lax
from jax.experimental import pallas as pl
from jax.experimental.pallas import tpu as pltpu

F32 = jnp.float32
BF16 = jnp.bfloat16

D_MODEL = 4096
DEPTH = 4
ATTN_WIDTH = D_MODEL // 2
HEAD_DIM = 64
N_Q_HEADS = ATTN_WIDTH // HEAD_DIM
N_KV_HEADS = N_Q_HEADS // 8
Q_PER_KV = N_Q_HEADS // N_KV_HEADS
KV_WIDTH = N_KV_HEADS * HEAD_DIM
WINDOW = 128
ROT_DIM = HEAD_DIM // 4
ROT_HALF = ROT_DIM // 2
ROPE_THETA = 500000.0
MASK_VALUE = -1e9
HGRN_WIDTH = D_MODEL - ATTN_WIDTH
HGRN_DIM = 128
N_HGRN_HEADS = HGRN_WIDTH // HGRN_DIM
HGRN_CHUNK = 64
HGRN_SUB = 16
HGRN_HEADS_PER_BLOCK = 4
HGRN_BLOCKS_PER_STEP = 2
HGRN_PIVOT_MAX_DECAY = 75.0
LRU_WIDTH = D_MODEL
N_LRU_HEADS = 16
LRU_HEAD_DIM = LRU_WIDTH // N_LRU_HEADS
CONV_WIDTH = 4
RG_C = 8.0
N_GROUPS = 4
EXPERTS_PER_GROUP = 8
N_EXPERTS = N_GROUPS * EXPERTS_PER_GROUP
TOP_K = 2
EXPERT_FF = 3 * D_MODEL // 32
EXPERT_TILE = 256
ALPHA = (2.0 * DEPTH) ** 0.25
LN_EPS = 1e-5
RMS_EPS = 1e-6

LANES = 128
SUBLANES = 8
VMEM_LIMIT = 56 * 1024 * 1024
NEG_BIG = -1e30

_NT = (((1,), (1,)), ((), ()))
_TN = (((0,), (0,)), ((), ()))


def _params(*sem):
    return pltpu.CompilerParams(dimension_semantics=sem, vmem_limit_bytes=VMEM_LIMIT)


def _mm_kernel(x_ref, w_ref, o_ref):
    o_ref[...] = jnp.dot(x_ref[...], w_ref[...].astype(BF16), preferred_element_type=F32).astype(o_ref.dtype)


def matmul(x, w, layer, out_dtype, tm=1024, tn=512):
    M, K = x.shape
    N = w.shape[2]
    tm = min(tm, M)
    return pl.pallas_call(
        _mm_kernel,
        grid=(M // tm, N // tn),
        in_specs=[pl.BlockSpec((tm, K), lambda i, j: (i, 0)),
                  pl.BlockSpec((None, K, tn), lambda i, j: (layer, 0, j))],
        out_specs=pl.BlockSpec((tm, tn), lambda i, j: (i, j)),
        out_shape=jax.ShapeDtypeStruct((M, N), out_dtype),
        compiler_params=_params("parallel", "arbitrary"),
        name="matmul",
    )(x, w)


LN_CHUNK = 512
HI_MASK = 0xFFFF0000


def _pack_bf16_pair(a, b):
    ua = lax.bitcast_convert_type(a.astype(BF16).astype(F32), jnp.uint32)
    ub = lax.bitcast_convert_type(b.astype(BF16).astype(F32), jnp.uint32)
    return (ua & jnp.uint32(HI_MASK)) | (ub >> 16)


def _unpack_bf16_pair(w):
    a = lax.bitcast_convert_type(w & jnp.uint32(HI_MASK), F32)
    b = lax.bitcast_convert_type(w << 16, F32)
    return a, b


def _layer_norm_inplace(of_ref, g_ref, b_ref, ob_ref=None, op_ref=None):
    D = of_ref.shape[1]
    assert D % (2 * LN_CHUNK) == 0
    cols = [slice(c, c + LN_CHUNK) for c in range(0, D, LN_CHUNK)]
    acc = None
    for sl in cols:
        acc = of_ref[:, sl] if acc is None else acc + of_ref[:, sl]
    mu = jnp.sum(acc, axis=-1, keepdims=True) * (1.0 / D)
    acc = None
    for sl in cols:
        d = of_ref[:, sl] - mu
        acc = d * d if acc is None else acc + d * d
    rstd = lax.rsqrt(jnp.sum(acc, axis=-1, keepdims=True) * (1.0 / D) + LN_EPS)
    def normalize(sl):
        y = (of_ref[:, sl] - mu) * rstd * g_ref[:, sl] + b_ref[:, sl]
        of_ref[:, sl] = y
        if ob_ref is not None:
            ob_ref[:, sl] = y.astype(BF16)
        return y

    for c in range(0, D // 2, LN_CHUNK):
        ya = normalize(slice(c, c + LN_CHUNK))
        yb = normalize(slice(D // 2 + c, D // 2 + c + LN_CHUNK))
        if op_ref is not None:
            op_ref[:, c:c + LN_CHUNK] = _pack_bf16_pair(ya, yb)


ROUTE_ROWS = 256


def _route(h, w_ref, b_ref):
    n = h.shape[0]
    h_hi = h.astype(BF16)
    h_lo = (h - h_hi.astype(F32)).astype(BF16)
    parts = jnp.dot(jnp.concatenate([h_hi, h_lo], axis=0), w_ref[...], preferred_element_type=F32)
    logits = (parts[:n, :LANES] + (parts[:n, LANES:] + parts[n:, :LANES]) + parts[n:, LANES:]) + b_ref[...]
    lane = lax.broadcasted_iota(jnp.int32, logits.shape, 1)
    neg = -jnp.inf
    big = jnp.int32(LANES)
    gl = jnp.where(lane < N_GROUPS, logits, neg)
    gmax = jnp.max(gl, axis=-1, keepdims=True)
    g_sel = jnp.min(jnp.where(gl == gmax, lane, big), axis=-1, keepdims=True)
    p_top = 1.0 / jnp.sum(jnp.exp(gl - gmax), axis=-1, keepdims=True)
    lo = N_GROUPS + g_sel * EXPERTS_PER_GROUP
    el = jnp.where((lane >= lo) & (lane < lo + EXPERTS_PER_GROUP), logits, neg)
    m1 = jnp.max(el, axis=-1, keepdims=True)
    i1 = jnp.min(jnp.where(el == m1, lane, big), axis=-1, keepdims=True)
    el2 = jnp.where(lane == i1, neg, el)
    m2 = jnp.max(el2, axis=-1, keepdims=True)
    i2 = jnp.min(jnp.where(el2 == m2, lane, big), axis=-1, keepdims=True)
    z = jnp.sum(jnp.exp(el - m1), axis=-1, keepdims=True)
    p1 = 1.0 / z
    p2 = jnp.exp(m2 - m1) / z
    tot = p1 + p2
    last = N_EXPERTS - 1
    idx = jnp.where(lane == 0, jnp.minimum(i1 - N_GROUPS, last), jnp.where(lane == 1, jnp.minimum(i2 - N_GROUPS, last), 0))
    wt = jnp.where(lane == 0, p1 / tot * p_top, jnp.where(lane == 1, p2 / tot * p_top, 0.0))
    return idx, wt


def _mm_ln_route_kernel(x_ref, w_ref, res_ref, g_ref, b_ref, rw_ref, rb_ref, of_ref, op_ref, idx_ref, wt_ref,
                        *, nj, tn):
    j = pl.program_id(1)
    pre = ALPHA * res_ref[...] + jnp.dot(x_ref[...], w_ref[...], preferred_element_type=F32)
    for c in range(nj):
        @pl.when(j == c)
        def _(c=c):
            of_ref[:, c * tn:(c + 1) * tn] = pre

    @pl.when(j == nj - 1)
    def _():
        _layer_norm_inplace(of_ref, g_ref, b_ref, op_ref=op_ref)

        def route_rows(r, carry):
            rows = pl.ds(pl.multiple_of(r * ROUTE_ROWS, ROUTE_ROWS), ROUTE_ROWS)
            idx_ref[rows, :], wt_ref[rows, :] = _route(of_ref[rows, :], rw_ref, rb_ref)
            return carry

        lax.fori_loop(0, of_ref.shape[0] // ROUTE_ROWS, route_rows, 0)


def matmul_residual_ln_route(x, w, res, g, b, rw, rb, layer, tm=512, tn=512):
    M, K = x.shape
    N = w.shape[2]
    tm = min(tm, M)
    nj = N // tn
    vec = pl.BlockSpec((None, 1, N), lambda i, j: (layer, 0, 0))
    return pl.pallas_call(
        functools.partial(_mm_ln_route_kernel, nj=nj, tn=tn),
        grid=(M // tm, nj),
        in_specs=[pl.BlockSpec((tm, K), lambda i, j: (i, 0)),
                  pl.BlockSpec((None, K, tn), lambda i, j: (layer // 2, 0, j)),
                  pl.BlockSpec((tm, tn), lambda i, j: (i, j)),
                  vec, vec,
                  pl.BlockSpec((None, N, 2 * LANES), lambda i, j: (layer, 0, 0)),
                  pl.BlockSpec((None, 1, LANES), lambda i, j: (layer, 0, 0))],
        out_specs=[pl.BlockSpec((tm, N), lambda i, j: (i, 0)),
                   pl.BlockSpec((tm, N // 2), lambda i, j: (i, 0)),
                   pl.BlockSpec((tm, LANES), lambda i, j: (i, 0)),
                   pl.BlockSpec((tm, LANES), lambda i, j: (i, 0))],
        out_shape=[jax.ShapeDtypeStruct((M, N), F32),
                   jax.ShapeDtypeStruct((M, N // 2), jnp.uint32),
                   jax.ShapeDtypeStruct((M, LANES), jnp.int32),
                   jax.ShapeDtypeStruct((M, LANES), F32)],
        compiler_params=_params("parallel", "arbitrary"),
        name="matmul_residual_ln_route",
    )(x, w, res, g, b, rw, rb)


def _rotary(x, tab):
    c, s_lo, s_hi = tab[:, :LANES], tab[:, LANES:2 * LANES], tab[:, 2 * LANES:]
    out = []
    for i in range(x.shape[1] // LANES):
        xc = x[:, i * LANES:(i + 1) * LANES]
        out.append(xc * c + pltpu.roll(xc, LANES - ROT_HALF, 1) * s_lo + pltpu.roll(xc, ROT_HALF, 1) * s_hi)
    return out


def _attn_kernel(sink_ref, q_ref, kc_ref, vc_ref, kp_ref, vp_ref, tc_ref, tp_ref, mask_ref, o_ref):
    W = WINDOW
    tab_c = tc_ref[...]
    tab_p = tp_ref[...]
    q_cols = _rotary(q_ref[...], tab_c)
    kc = jnp.concatenate(_rotary(kc_ref[...], tab_c), axis=1)
    kp = jnp.concatenate(_rotary(kp_ref[...], tab_p), axis=1)
    k_all = jnp.concatenate([kp, kc], axis=0).astype(BF16)
    v_all = jnp.concatenate([vp_ref[...], vc_ref[...]], axis=0).astype(BF16)
    outs = []
    for kv in range(N_KV_HEADS):
        heads = range(kv * Q_PER_KV, (kv + 1) * Q_PER_KV)
        qg = jnp.concatenate(
            [q_cols[h // 2][:, (h % 2) * HEAD_DIM:(h % 2 + 1) * HEAD_DIM] for h in heads], axis=0).astype(BF16)
        kh = k_all[:, kv * HEAD_DIM:(kv + 1) * HEAD_DIM]
        vh = v_all[:, kv * HEAD_DIM:(kv + 1) * HEAD_DIM]
        s_all = lax.dot_general(qg, kh, _NT, preferred_element_type=F32)
        ps, denoms = [], []
        for g, h in enumerate(heads):
            s = s_all[g * W:(g + 1) * W] * mask_ref[:, :2 * W] + mask_ref[:, 2 * W:4 * W]
            sink = sink_ref[h]
            m = jnp.maximum(jnp.max(s, axis=-1, keepdims=True), sink)
            p = jnp.exp(s - m) * mask_ref[:, 4 * W:]
            denoms.append(jnp.sum(p, axis=-1, keepdims=True) + jnp.exp(sink - m))
            ps.append(p.astype(BF16))
        o_all = jnp.dot(jnp.concatenate(ps, axis=0), vh, preferred_element_type=F32)
        for g in range(Q_PER_KV):
            outs.append(o_all[g * W:(g + 1) * W] / denoms[g])
    o_ref[...] = jnp.concatenate(outs, axis=1).astype(o_ref.dtype)


def _band_masks():
    qi = jnp.arange(WINDOW)[:, None]
    kj = jnp.arange(2 * WINDOW)[None, :]
    rel = qi + WINDOW - kj
    in_band = (rel >= 0) & (rel < WINDOW)
    out = []
    for has_prev in (False, True):
        keep = (in_band & ((kj >= WINDOW) | has_prev)).astype(F32)
        out.append(jnp.concatenate([keep * (HEAD_DIM ** -0.5), (1.0 - keep) * MASK_VALUE, keep], axis=1))
    return jnp.stack(out)


def swa_attention(proj, tab, sinks, batch, seq):
    nb = seq // WINDOW
    kcol = ATTN_WIDTH // KV_WIDTH
    cur = lambda b, n: b * nb + n
    prev = lambda b, n: b * nb + jnp.maximum(n - 1, 0)
    return pl.pallas_call(
        _attn_kernel,
        grid=(batch, nb),
        in_specs=[pl.BlockSpec(memory_space=pltpu.SMEM),
                  pl.BlockSpec((WINDOW, ATTN_WIDTH), lambda b, n: (cur(b, n), 0)),
                  pl.BlockSpec((WINDOW, KV_WIDTH), lambda b, n: (cur(b, n), kcol)),
                  pl.BlockSpec((WINDOW, KV_WIDTH), lambda b, n: (cur(b, n), kcol + 1)),
                  pl.BlockSpec((WINDOW, KV_WIDTH), lambda b, n: (prev(b, n), kcol)),
                  pl.BlockSpec((WINDOW, KV_WIDTH), lambda b, n: (prev(b, n), kcol + 1)),
                  pl.BlockSpec((WINDOW, 3 * LANES), lambda b, n: (cur(b, n), 0)),
                  pl.BlockSpec((WINDOW, 3 * LANES), lambda b, n: (prev(b, n), 0)),
                  pl.BlockSpec((None, WINDOW, 6 * WINDOW), lambda b, n: (jnp.minimum(n, 1), 0, 0))],
        out_specs=pl.BlockSpec((WINDOW, ATTN_WIDTH), lambda b, n: (cur(b, n), 0)),
        out_shape=jax.ShapeDtypeStruct((batch * seq, ATTN_WIDTH), BF16),
        compiler_params=_params("parallel", "parallel"),
        name="swa_attention",
    )(sinks, proj, proj, proj, proj, proj, tab, tab, _band_masks())


def _hgrn_gates(q_in, z, lb):
    C = HGRN_CHUNK
    sg = jax.nn.sigmoid(z)
    lf = jnp.log(lb + (1.0 - lb) * sg)
    k = (1.0 - lb) * jax.nn.sigmoid(-z)
    q = q_in * jax.nn.sigmoid(q_in)
    ti = lax.broadcasted_iota(jnp.int32, (C, C), 0)
    si = lax.broadcasted_iota(jnp.int32, (C, C), 1)
    tril = jnp.where(ti >= si, 1.0, 0.0).astype(F32)
    b = jnp.dot(tril, lf, preferred_element_type=F32, precision=lax.Precision.HIGHEST)
    return q, k, b


def _hgrn_intra_pivot(q, k, v, b):
    C = HGRN_CHUNK
    sc = lax.dot_general((q * jnp.exp(b)).astype(BF16), (k * jnp.exp(-b)).astype(BF16), _NT,
                         preferred_element_type=F32)
    ti = lax.broadcasted_iota(jnp.int32, (C, C), 0)
    si = lax.broadcasted_iota(jnp.int32, (C, C), 1)
    sc = jnp.where(ti >= si, sc, 0.0)
    return jnp.dot(sc.astype(BF16), v.astype(BF16), preferred_element_type=F32)


def _hgrn_intra_safe(q, k, v, b):
    C, c = HGRN_CHUNK, HGRN_SUB
    half = c // 2
    rows = lax.broadcasted_iota(jnp.int32, (half, HGRN_DIM), 0)
    pieces = [None] * (C // c)
    for j in range(C // c):
        lo, hi = j * c, (j + 1) * c
        bd, qd, kd, vd = b[lo:hi], q[lo:hi], k[lo:hi], v[lo:hi]
        e_j = b[hi - 1:hi, :]
        if hi < C:
            khat = (kd * jnp.exp(e_j - bd)).astype(BF16)
            qt = (q[hi:] * jnp.exp(b[hi:] - e_j)).astype(BF16)
            sc = lax.dot_general(qt, khat, _NT, preferred_element_type=F32)
            upd = jnp.dot(sc.astype(BF16), vd.astype(BF16), preferred_element_type=F32)
            for jj in range(j + 1, C // c):
                u = upd[(jj - j - 1) * c:(jj - j) * c]
                pieces[jj] = u if pieces[jj] is None else pieces[jj] + u
        acc = [jnp.zeros((half, HGRN_DIM), F32), jnp.zeros((half, HGRN_DIM), F32)]
        for s in range(c):
            for hf in range(s // half, 2):
                r0 = hf * half
                diff = bd[r0:r0 + half] - bd[s:s + 1]
                if s >= r0:
                    diff = jnp.where(rows >= s - r0, diff, NEG_BIG)
                col = jnp.sum(qd[r0:r0 + half] * kd[s:s + 1] * jnp.exp(diff), axis=-1, keepdims=True)
                acc[hf] = acc[hf] + col * vd[s:s + 1]
        d = jnp.concatenate(acc, axis=0)
        pieces[j] = d if pieces[j] is None else pieces[j] + d
    return jnp.concatenate(pieces, axis=0)


def _hgrn_finish(o_intra, q, k, v, b, g_in, gn, state_ref):
    C = HGRN_CHUNK
    b_last = b[C - 1:C, :]
    state = state_ref[...]
    o = o_intra + lax.dot_general((q * jnp.exp(b)).astype(BF16), state.astype(BF16), _NT,
                                  preferred_element_type=F32)
    kdec = (k * jnp.exp(b_last - b)).astype(BF16)
    state_ref[...] = state * jnp.exp(b_last) + lax.dot_general(v.astype(BF16), kdec, _TN, preferred_element_type=F32)
    ms = jnp.mean(o * o, axis=-1, keepdims=True)
    return o * lax.rsqrt(ms + RMS_EPS) * gn * (g_in * jax.nn.sigmoid(g_in))


def _hgrn_kernel(*refs, n_chunks):
    nb, hb = HGRN_BLOCKS_PER_STEP, HGRN_HEADS_PER_BLOCK
    q_refs, f_refs, i_refs, g_refs, lb_refs = (refs[k * nb:(k + 1) * nb] for k in range(5))
    gn_ref = refs[5 * nb]
    o_refs = refs[5 * nb + 1:6 * nb + 1]
    state_ref = refs[6 * nb + 1]

    @pl.when(pl.program_id(2) == 0)
    def _():
        state_ref[...] = jnp.zeros_like(state_ref)

    gn = gn_ref[...]
    heads = [(blk, slice(hd * HGRN_DIM, (hd + 1) * HGRN_DIM)) for blk in range(nb) for hd in range(hb)]

    def body(ci, carry):
        r0 = pl.multiple_of(ci * HGRN_CHUNK, HGRN_CHUNK)
        rs = pl.ds(r0, HGRN_CHUNK)
        gates = [_hgrn_gates(q_refs[blk][rs, cs], f_refs[blk][rs, cs], lb_refs[blk][:, cs]) for blk, cs in heads]
        total = gates[0][2][HGRN_CHUNK - 1:HGRN_CHUNK, :]
        for _, _, b in gates[1:]:
            total = jnp.minimum(total, b[HGRN_CHUNK - 1:HGRN_CHUNK, :])
        mild = jnp.min(total) >= -HGRN_PIVOT_MAX_DECAY

        def run(intra):
            for n, (blk, cs) in enumerate(heads):
                q, k, b = gates[n]
                v = i_refs[blk][rs, cs]
                y = _hgrn_finish(intra(q, k, v, b), q, k, v, b, g_refs[blk][rs, cs], gn, state_ref.at[n])
                o_refs[blk][rs, cs] = y.astype(o_refs[blk].dtype)

        pl.when(mild)(lambda: run(_hgrn_intra_pivot))
        pl.when(jnp.logical_not(mild))(lambda: run(_hgrn_intra_safe))
        return carry

    lax.fori_loop(0, n_chunks, body, 0)


def hgrn2(proj, lb, gnorm, layer, batch, seq, tile=512):
    tile = min(tile, seq)
    nt = seq // tile
    nb = HGRN_BLOCKS_PER_STEP
    w = HGRN_HEADS_PER_BLOCK * HGRN_DIM
    c0 = (ATTN_WIDTH + 2 * KV_WIDTH) // w
    per_kind = HGRN_WIDTH // w
    per_slab = per_kind // nb
    spec = lambda k, blk: pl.BlockSpec(
        (tile, w), lambda b, h, t: (b * nt + t, c0 + k * per_kind + blk * per_slab + h))
    blocks = range(nb)
    return pl.pallas_call(
        functools.partial(_hgrn_kernel, n_chunks=tile // HGRN_CHUNK),
        grid=(batch, per_slab, nt),
        in_specs=([spec(k, blk) for k in range(4) for blk in blocks]
                  + [pl.BlockSpec((None, 1, w), lambda b, h, t, blk=blk: (layer, 0, blk * per_slab + h))
                     for blk in blocks]
                  + [pl.BlockSpec((None, 1, HGRN_DIM), lambda b, h, t: (layer, 0, 0))]),
        out_specs=[pl.BlockSpec((tile, w), lambda b, h, t: (b * nt + t, h)) for _ in blocks],
        out_shape=[jax.ShapeDtypeStruct((batch * seq, HGRN_WIDTH // nb), BF16) for _ in blocks],
        scratch_shapes=[pltpu.VMEM((nb * HGRN_HEADS_PER_BLOCK, HGRN_DIM, HGRN_DIM), F32)],
        compiler_params=_params("parallel", "parallel", "arbitrary"),
        name="hgrn2",
    )(*([proj] * (4 * nb)), *([lb] * nb), gnorm)


def _lru_kernel(gate_ref, rnn_ref, cw_ref, cb_ref, wa_ref, wx_ref, ba_ref, bx_ref, lam_ref, o_ref,
                xs_ref, h_ref, *, tile):
    @pl.when(pl.program_id(2) == 0)
    def _():
        xs_ref[0:SUBLANES, :] = jnp.zeros((SUBLANES, LRU_HEAD_DIM), F32)
        h_ref[...] = jnp.zeros_like(h_ref)

    x = rnn_ref[...]
    xs_ref[SUBLANES:, :] = x
    u = cb_ref[...] + x * cw_ref[CONV_WIDTH - 1:CONV_WIDTH, :]
    for j in range(CONV_WIDTH - 1):
        back = CONV_WIDTH - 1 - j
        u = u + xs_ref[pl.ds(SUBLANES - back, tile), :] * cw_ref[j:j + 1, :]
    xs_ref[0:SUBLANES, :] = x[tile - SUBLANES:, :]

    ub = u.astype(BF16)
    r = jax.nn.sigmoid(jnp.dot(ub, wa_ref[...], preferred_element_type=F32) + ba_ref[...])
    ig = jax.nn.sigmoid(jnp.dot(ub, wx_ref[...], preferred_element_type=F32) + bx_ref[...])
    nl = -lam_ref[...]
    softplus = jnp.maximum(nl, 0.0) + jnp.log1p(jnp.exp(-jnp.abs(nl)))
    log_a = -RG_C * r * softplus
    a = jnp.exp(log_a)
    bb = jnp.sqrt(jnp.maximum(-jnp.tanh(log_a) * (a * a + 1.0), 0.0)) * (ig * u)

    rows = lax.broadcasted_iota(jnp.int32, (tile, LRU_HEAD_DIM), 0)
    k = 1
    while k < tile:
        keep = rows >= k
        a_sh = jnp.where(keep, pltpu.roll(a, k, 0), 1.0)
        b_sh = jnp.where(keep, pltpu.roll(bb, k, 0), 0.0)
        bb = a * b_sh + bb
        a = a * a_sh
        k *= 2
    h = a * h_ref[...] + bb
    h_ref[...] = h[tile - 1:tile, :]
    o_ref[...] = (h * jax.nn.gelu(gate_ref[...])).astype(o_ref.dtype)


def recurrent_block(proj, conv_w, conv_b, wa, ba, wx, bx, lam, layer, batch, seq, tile=256):
    tile = min(tile, seq)
    nt = seq // tile
    hd = LRU_HEAD_DIM
    vec = pl.BlockSpec((None, 1, hd), lambda b, h, t: (layer, 0, h))
    mat = pl.BlockSpec((None, None, hd, hd), lambda b, h, t: (layer, h, 0, 0))
    return pl.pallas_call(
        functools.partial(_lru_kernel, tile=tile),
        grid=(batch, N_LRU_HEADS, nt),
        in_specs=[pl.BlockSpec((tile, hd), lambda b, h, t: (b * nt + t, h)),
                  pl.BlockSpec((tile, hd), lambda b, h, t: (b * nt + t, N_LRU_HEADS + h)),
                  pl.BlockSpec((None, CONV_WIDTH, hd), lambda b, h, t: (layer, 0, h)),
                  vec, mat, mat, vec, vec, vec],
        out_specs=pl.BlockSpec((tile, hd), lambda b, h, t: (b * nt + t, h)),
        out_shape=jax.ShapeDtypeStruct((batch * seq, LRU_WIDTH), BF16),
        scratch_shapes=[pltpu.VMEM((tile + SUBLANES, hd), F32), pltpu.VMEM((1, hd), F32)],
        compiler_params=_params("parallel", "parallel", "arbitrary"),
        name="recurrent_block",
    )(proj, proj, conv_w, conv_b, wa, wx, ba, bx, lam)


def _expert_kernel(te_ref, tok_ref, nused_ref, h_hbm, wg_ref, wu_ref, wd_ref, y_ref, xbuf, sem):
    t = pl.program_id(0)
    nused = nused_ref[0]
    T = EXPERT_TILE

    def gather(tile, slot):
        base = tile * T
        for r in range(T):
            tok = tok_ref[base + r]
            pltpu.make_async_copy(h_hbm.at[pl.ds(pl.multiple_of(tok * 16, 16), 16)], xbuf.at[slot, pl.ds(r * 20, 16)], sem.at[slot]).start()

    def wait(slot):
        pltpu.make_async_copy(h_hbm.at[pl.ds(0, T * 16)], xbuf.at[slot, pl.ds(0, T * 16)], sem.at[slot]).wait()

    @pl.when(t == 0)
    def _():
        gather(0, 0)

    slot = lax.rem(t, 2)

    @pl.when(t < nused)
    def _():
        wait(slot)
        gather(jnp.minimum(t + 1, nused - 1), 1 - slot)
        parts = [_unpack_bf16_pair(xbuf[slot, pl.ds(c, T, stride=20), :]) for c in range(16)]
        x = jnp.concatenate([p[0].astype(BF16) for p in parts] + [p[1].astype(BF16) for p in parts], axis=1)
        gate = jnp.dot(x, wg_ref[...].astype(BF16), preferred_element_type=F32)
        up = jnp.dot(x, wu_ref[...].astype(BF16), preferred_element_type=F32)
        hid = (gate * jax.nn.sigmoid(gate) * up).astype(BF16)
        y = jnp.dot(hid, wd_ref[...].astype(BF16), preferred_element_type=F32)
        half = y.shape[1] // 2
        y_ref[...] = _pack_bf16_pair(y[:, :half], y[:, half:])

        @pl.when(t == nused - 1)
        def _():
            wait(1 - slot)

    @pl.when(t >= nused)
    def _():
        y_ref[...] = jnp.zeros_like(y_ref)


def moe_experts(h, wg, wu, wd, tile_e, slot_tok, n_used, layer):
    n_tiles = tile_e.shape[0]
    T = EXPERT_TILE
    D = wg.shape[2]
    grid_spec = pltpu.PrefetchScalarGridSpec(
        num_scalar_prefetch=3,
        grid=(n_tiles,),
        in_specs=[pl.BlockSpec(memory_space=pl.ANY),
                  pl.BlockSpec((None, None, D, EXPERT_FF), lambda i, te, tok, nu: (layer, te[i], 0, 0)),
                  pl.BlockSpec((None, None, D, EXPERT_FF), lambda i, te, tok, nu: (layer, te[i], 0, 0)),
                  pl.BlockSpec((None, None, EXPERT_FF, D), lambda i, te, tok, nu: (layer, te[i], 0, 0))],
        out_specs=pl.BlockSpec((T, D // 2), lambda i, te, tok, nu: (i, 0)),
        scratch_shapes=[pltpu.VMEM((2, T * 20, LANES), jnp.uint32), pltpu.SemaphoreType.DMA((2,))],
    )
    return pl.pallas_call(
        _expert_kernel,
        grid_spec=grid_spec,
        out_shape=jax.ShapeDtypeStruct((n_tiles * T, D // 2), jnp.uint32),
        compiler_params=_params("arbitrary"),
        name="moe_experts",
    )(tile_e, slot_tok, n_used, h.reshape(-1, LANES), wg, wu, wd)


def _combine_ln_kernel(dest_ref, y_hbm, h_ref, wt_ref, g_ref, b_ref, of_ref, ob_ref, ybuf, sem, *, tm, nsteps):
    t = pl.program_id(0)

    def gather(step, slot):
        base = step * (tm * TOP_K)
        for r in range(tm):
            for k in range(TOP_K):
                d = dest_ref[base + r * TOP_K + k]
                pltpu.make_async_copy(y_hbm.at[pl.ds(d, 1)], ybuf.at[slot, k, pl.ds(r, 1)], sem.at[slot]).start(priority=k)

    def wait(slot):
        for k in range(TOP_K):
            pltpu.make_async_copy(y_hbm.at[pl.ds(0, tm)], ybuf.at[slot, k], sem.at[slot]).wait()

    @pl.when(t == 0)
    def _():
        gather(0, 0)

    slot = lax.rem(t, 2)
    wait(slot)
    gather(jnp.minimum(t + 1, nsteps - 1), 1 - slot)
    w0 = wt_ref[:, 0:1]
    w1 = wt_ref[:, 1:2]
    half = h_ref.shape[1] // 2
    for c in range(0, half, LN_CHUNK):
        sl_a, sl_b = slice(c, c + LN_CHUNK), slice(half + c, half + c + LN_CHUNK)
        y0a, y0b = _unpack_bf16_pair(ybuf[slot, 0, :, sl_a])
        y1a, y1b = _unpack_bf16_pair(ybuf[slot, 1, :, sl_a])
        of_ref[:, sl_a] = ALPHA * h_ref[:, sl_a] + (y0a * w0 + y1a * w1)
        of_ref[:, sl_b] = ALPHA * h_ref[:, sl_b] + (y0b * w0 + y1b * w1)
    _layer_norm_inplace(of_ref, g_ref, b_ref, ob_ref)

    @pl.when(t == nsteps - 1)
    def _():
        wait(1 - slot)


def moe_combine_ln(y, h, wts, dest, g, b, layer, tm=256):
    N = wts.shape[0]
    D = g.shape[2]
    tm = min(tm, N)
    nsteps = N // tm
    vec = pl.BlockSpec((None, 1, D), lambda i, d: (layer, 0, 0))
    grid_spec = pltpu.PrefetchScalarGridSpec(
        num_scalar_prefetch=1,
        grid=(nsteps,),
        in_specs=[pl.BlockSpec(memory_space=pl.ANY),
                  pl.BlockSpec((tm, D), lambda i, d: (i, 0)),
                  pl.BlockSpec((tm, LANES), lambda i, d: (i, 0)),
                  vec, vec],
        out_specs=[pl.BlockSpec((tm, D), lambda i, d: (i, 0)),
                   pl.BlockSpec((tm, D), lambda i, d: (i, 0))],
        scratch_shapes=[pltpu.VMEM((2, TOP_K, tm, D // 2), jnp.uint32), pltpu.SemaphoreType.DMA((2,))],
    )
    return pl.pallas_call(
        functools.partial(_combine_ln_kernel, tm=tm, nsteps=nsteps),
        grid_spec=grid_spec,
        out_shape=[jax.ShapeDtypeStruct((N, D), F32), jax.ShapeDtypeStruct((N, D), BF16)],
        compiler_params=_params("arbitrary"),
        name="moe_combine_ln",
    )(dest, y, h, wts, g, b)


def _dispatch_plan(e_idx):
    N = e_idx.shape[0]
    A = N * TOP_K
    T = EXPERT_TILE
    n_tiles = (A + N_EXPERTS * (T - 1) + T - 1) // T
    flat_e = e_idx.reshape(-1)
    onehot = jax.nn.one_hot(flat_e, N_EXPERTS, dtype=jnp.int32)
    counts = jnp.sum(onehot, axis=0)
    rank = jnp.sum((jnp.cumsum(onehot, axis=0) - 1) * onehot, axis=1)
    padded = (counts + T - 1) // T * T
    pad_end = jnp.cumsum(padded)
    dest = (pad_end[flat_e] - padded[flat_e] + rank).astype(jnp.int32)
    slot_tok = jnp.zeros((n_tiles * T,), jnp.int32).at[dest].set(jnp.arange(A, dtype=jnp.int32) // TOP_K)
    starts = jnp.arange(n_tiles, dtype=jnp.int32) * T
    tile_e = jnp.minimum(jnp.sum((pad_end[None, :] <= starts[:, None]).astype(jnp.int32), axis=1), N_EXPERTS - 1)
    n_used = (pad_end[-1] // T).astype(jnp.int32).reshape(1)
    return tile_e.astype(jnp.int32), slot_tok, n_used, dest


def _rotary_table(positions):
    inv_freq = ROPE_THETA ** (-jnp.arange(0, ROT_DIM, 2, dtype=F32) / ROT_DIM)
    ang = positions.astype(F32).reshape(-1)[:, None] * inv_freq
    cos, sin = jnp.cos(ang), jnp.sin(ang)
    n = ang.shape[0]
    rest = HEAD_DIM - ROT_DIM
    z = lambda w: jnp.zeros((n, w), F32)
    c = jnp.concatenate([cos, cos, jnp.ones((n, rest), F32)], axis=1)
    s_lo = jnp.concatenate([-sin, z(HEAD_DIM - ROT_HALF)], axis=1)
    s_hi = jnp.concatenate([z(ROT_HALF), sin, z(rest)], axis=1)
    rep = LANES // HEAD_DIM
    return jnp.concatenate([jnp.tile(c, (1, rep)), jnp.tile(s_lo, (1, rep)), jnp.tile(s_hi, (1, rep))], axis=1)


def kernel(x, positions, even_w_in, even_w_out, attn_sinks, hgrn_lb_logits, hgrn_gnorm_w, rec_w_in, rec_conv_w, rec_conv_b, rec_gate_a_w, rec_gate_a_b, rec_gate_x_w, rec_gate_x_b, rec_lambda, rec_w_out, ln_mix_g, ln_mix_b, ln_ffn_g, ln_ffn_b, router_group_w, router_group_b, router_expert_w, router_expert_b, moe_w_gate, moe_w_up, moe_w_down):
    B, S, D = x.shape
    N = B * S
    tab = _rotary_table(positions)
    sm = jax.nn.softmax(hgrn_lb_logits.astype(F32), axis=0)
    lb_table = (jnp.cumsum(sm, axis=0) - sm[:1])[:, None, :]
    pad = LANES - N_GROUPS - N_EXPERTS
    router_w = jnp.concatenate([router_group_w, router_expert_w, jnp.zeros((DEPTH, D, pad), F32)], axis=-1)
    router_b = jnp.concatenate([router_group_b, router_expert_b, jnp.zeros((DEPTH, pad), F32)], axis=-1)[:, None, :]
    router_w_hi = router_w.astype(BF16)
    router_w_lo = (router_w - router_w_hi.astype(F32)).astype(BF16)
    router_w2 = jnp.concatenate([router_w_hi, router_w_lo], axis=-1)
    row = lambda a: a[:, None, :]
    even_w_out_b, rec_w_out_b = even_w_out.astype(BF16), rec_w_out.astype(BF16)
    wa_b, wx_b = rec_gate_a_w.astype(BF16), rec_gate_x_w.astype(BF16)
    gnorm = row(hgrn_gnorm_w)
    ln_mix_g3, ln_mix_b3, ln_ffn_g3, ln_ffn_b3 = row(ln_mix_g), row(ln_mix_b), row(ln_ffn_g), row(ln_ffn_b)

    h = x.reshape(N, D)
    hb = h.astype(BF16)
    for layer in range(DEPTH):
        j = layer // 2
        if layer % 2 == 0:
            proj = matmul(hb, even_w_in, j, F32)
            o_a = swa_attention(proj, tab, attn_sinks[j], B, S)
            o_b = hgrn2(proj, lb_table, gnorm, j, B, S)
            mixed = jnp.concatenate([o_a, *o_b], axis=1)
            w_out = even_w_out_b
        else:
            proj = matmul(hb, rec_w_in, j, F32)
            mixed = recurrent_block(proj, rec_conv_w, row(rec_conv_b), wa_b, row(rec_gate_a_b), wx_b,
                                    row(rec_gate_x_b), row(rec_lambda), j, B, S)
            w_out = rec_w_out_b
        h, hp, idx, wts = matmul_residual_ln_route(mixed, w_out, h, ln_mix_g3, ln_mix_b3, router_w2, router_b, layer)
        tile_e, slot_tok, n_used, dest = _dispatch_plan(idx[:, :TOP_K])
        y = moe_experts(hp, moe_w_gate, moe_w_up, moe_w_down, tile_e, slot_tok, n_used, layer)
        h, hb = moe_combine_ln(y, h, wts, dest, ln_ffn_g3, ln_ffn_b3, layer)
    return h.reshape(B, S, D)
```
